```python
import jax, jax.numpy as jnp
from jax import lax
import numpy as np

D_MODEL = 1024
BATCH = 4
SEQ = 4096
DEPTH = 1

LRU_WIDTH = D_MODEL // 2
LRU_BLOCKS = 8
LRU_BLOCK_DIM = LRU_WIDTH // LRU_BLOCKS
CONV_WIDTH = 4
RG_C = 8.0
LRU_MIN_RAD = 0.9
LRU_MAX_RAD = 0.999
RET_HEADS = 8
RET_HEAD_DIM = 64
RET_WIDTH = RET_HEADS * RET_HEAD_DIM
CHUNK = 128
ROPE_BASE = 10000.0
D_FF = 4 * D_MODEL
EPS = 1e-6
IN_SPLITS = (LRU_WIDTH, LRU_WIDTH, RET_WIDTH, RET_WIDTH, RET_WIDTH, RET_WIDTH, D_MODEL, D_MODEL)
D_IN = sum(IN_SPLITS)

kernel_name = "hybrid_rglru_retention_gated_block"


def rmsnorm(x, g):
    xf = x.astype(jnp.float32)
    y = xf * lax.rsqrt(jnp.mean(xf * xf, axis=-1, keepdims=True) + EPS)
    return (y * g.astype(jnp.float32)).astype(x.dtype)


def causal_depthwise_conv(x, w, b):
    c = x.shape[-1]
    y = lax.conv_general_dilated(x, w[:, None, :], window_strides=(1,),
                                 padding=[(CONV_WIDTH - 1, 0)],
                                 dimension_numbers=("NWC", "WIO", "NWC"),
                                 feature_group_count=c)
    return y + b


def rg_lru(x, w_r, b_r, w_i, b_i, lam):
    bsz, s, w = x.shape
    xb = x.reshape(bsz, s, LRU_BLOCKS, LRU_BLOCK_DIM)
    r = jax.nn.sigmoid(jnp.einsum("bshd,hde->bshe", xb, w_r) + b_r).reshape(bsz, s, w)
    i = jax.nn.sigmoid(jnp.einsum("bshd,hde->bshe", xb, w_i) + b_i).reshape(bsz, s, w)
    log_a = -RG_C * r.astype(jnp.float32) * jax.nn.softplus(-lam.astype(jnp.float32))
    a = jnp.exp(log_a)
    mult = jnp.sqrt(-jnp.expm1(2.0 * log_a))
    u = mult * (i * x).astype(jnp.float32)

    def combine(c1, c2):
        a1, b1 = c1
        a2, b2 = c2
        return a1 * a2, a2 * b1 + b2

    _, h = lax.associative_scan(combine, (a, u), axis=1)
    return h.astype(x.dtype)


def rotary(x, pos):
    half = x.shape[-1] // 2
    inv = ROPE_BASE ** (-jnp.arange(half, dtype=jnp.float32) / half)
    ang = pos[:, None] * inv[None, :]
    cos = jnp.cos(ang)[None, :, None, :]
    sin = jnp.sin(ang)[None, :, None, :]
    xf = x.astype(jnp.float32)
    x1, x2 = xf[..., :half], xf[..., half:]
    return jnp.concatenate([x1 * cos - x2 * sin, x1 * sin + x2 * cos], axis=-1).astype(x.dtype)


def chunkwise_retention(q, k, v):
    bsz, s, nh, dh = q.shape
    n = s // CHUNK
    log_g = jnp.log1p(-jnp.exp2(-5.0 - jnp.arange(nh, dtype=jnp.float32)))
    pos = jnp.arange(CHUNK, dtype=jnp.float32)
    rel = pos[:, None] - pos[None, :]
    intra_decay = jnp.where(rel[None] >= 0,
                            jnp.exp(log_g[:, None, None] * jnp.maximum(rel, 0.0)[None]), 0.0)
    q_decay = jnp.exp(log_g[:, None] * (pos[None, :] + 1.0))
    k_decay = jnp.exp(log_g[:, None] * (CHUNK - 1.0 - pos[None, :]))
    chunk_decay = jnp.exp(log_g * CHUNK)
    qc = q.astype(jnp.float32).reshape(bsz, n, CHUNK, nh, dh)
    kc = k.astype(jnp.float32).reshape(bsz, n, CHUNK, nh, dh)
    vc = v.astype(jnp.float32).reshape(bsz, n, CHUNK, nh, dh)
    scores = jnp.einsum("bnchd,bnmhd->bnhcm", qc, kc) * intra_decay[None, None]
    intra = jnp.einsum("bnhcm,bnmhe->bnche", scores, vc)
    kv = jnp.einsum("bnchd,hc,bnche->bnhde", kc, k_decay, vc)

    def step(state, kv_i):
        return chunk_decay[None, :, None, None] * state + kv_i, state

    _, states = lax.scan(step, jnp.zeros((bsz, nh, dh, dh), jnp.float32), jnp.moveaxis(kv, 1, 0))
    states = jnp.moveaxis(states, 0, 1)
    inter = jnp.einsum("bnchd,hc,bnhde->bnche", qc, q_decay, states)
    return (intra + inter).reshape(bsz, s, nh, dh)


def setup_inputs(seed: int = 0) -> dict:
    key = jax.random.key(seed)
    ks = jax.random.split(key, 20)
    L = DEPTH

    def nrm(k, shape, fan_in):
        return jax.random.normal(k, shape, jnp.float32) * fan_in ** -0.5

    u = jax.random.uniform(ks[9], (L, LRU_WIDTH), jnp.float32)
    rad2 = u * (LRU_MAX_RAD ** 2 - LRU_MIN_RAD ** 2) + LRU_MIN_RAD ** 2
    a_real = 0.5 * jnp.log(rad2)
    lru_lambda = -jnp.log(jnp.expm1(-a_real))
    return {
        "x": jax.random.normal(ks[0], (BATCH, SEQ, D_MODEL), jnp.float32),
        "norm1_g": 1.0 + 0.05 * jax.random.normal(ks[1], (L, D_MODEL), jnp.float32),
        "w_in": nrm(ks[2], (L, D_MODEL, D_IN), D_MODEL),
        "conv_w": nrm(ks[3], (L, CONV_WIDTH, LRU_WIDTH), CONV_WIDTH),
        "conv_b": 0.02 * jax.random.normal(ks[4], (L, LRU_WIDTH), jnp.float32),
        "lru_wr": nrm(ks[5], (L, LRU_BLOCKS, LRU_BLOCK_DIM, LRU_BLOCK_DIM), LRU_BLOCK_DIM),
        "lru_br": 0.02 * jax.random.normal(ks[6], (L, LRU_BLOCKS, LRU_BLOCK_DIM), jnp.float32),
        "lru_wi": nrm(ks[7], (L, LRU_BLOCKS, LRU_BLOCK_DIM, LRU_BLOCK_DIM), LRU_BLOCK_DIM),
        "lru_bi": 0.02 * jax.random.normal(ks[8], (L, LRU_BLOCKS, LRU_BLOCK_DIM), jnp.float32),
        "lru_lambda": lru_lambda,
        "w_branch_a": nrm(ks[10], (L, LRU_WIDTH, D_MODEL), LRU_WIDTH),
        "w_branch_b": nrm(ks[11], (L, RET_WIDTH, D_MODEL), RET_WIDTH),
        "w_out": nrm(ks[12], (L, D_MODEL, D_MODEL), D_MODEL),
        "norm2_g": 1.0 + 0.05 * jax.random.normal(ks[13], (L, D_MODEL), jnp.float32),
        "w_ff1": nrm(ks[14], (L, D_MODEL, D_FF), D_MODEL),
        "w_ff2": nrm(ks[15], (L, D_FF, D_MODEL), D_FF),
        "norm_f_g": 1.0 + 0.05 * jax.random.normal(ks[16], (D_MODEL,), jnp.float32),
    }


def reference(x, norm1_g, w_in, conv_w, conv_b, lru_wr, lru_br, lru_wi, lru_bi, lru_lambda,
              w_branch_a, w_branch_b, w_out, norm2_g, w_ff1, w_ff2, norm_f_g):
    bsz, s, _ = x.shape
    pos = jnp.arange(s, dtype=jnp.float32)
    offsets = np.cumsum(np.array(IN_SPLITS))[:-1]
    for l in range(DEPTH):
        h = rmsnorm(x, norm1_g[l])
        proj = h @ w_in[l]
        xa, ga, q, k, v, gr, sa, sb = jnp.split(proj, offsets, axis=-1)
        xa = causal_depthwise_conv(xa, conv_w[l], conv_b[l])
        ya = jax.nn.gelu(ga) * rg_lru(xa, lru_wr[l], lru_br[l], lru_wi[l], lru_bi[l], lru_lambda[l])
        q = rotary(q.reshape(bsz, s, RET_HEADS, RET_HEAD_DIM), pos)
        k = rotary(k.reshape(bsz, s, RET_HEADS, RET_HEAD_DIM), pos) * (RET_HEAD_DIM ** -0.5)
        v = v.reshape(bsz, s, RET_HEADS, RET_HEAD_DIM)
        ret = chunkwise_retention(q, k, v)
        ret = ret * lax.rsqrt(jnp.mean(ret * ret, axis=-1, keepdims=True) + EPS)
        yb = jax.nn.silu(gr) * ret.reshape(bsz, s, RET_WIDTH).astype(x.dtype)
        m = jax.nn.sigmoid(sa) * (ya @ w_branch_a[l]) + jax.nn.sigmoid(sb) * (yb @ w_branch_b[l])
        x = x + m @ w_out[l]
        h2 = rmsnorm(x, norm2_g[l])
        x = x + jnp.square(jax.nn.relu(h2 @ w_ff1[l])) @ w_ff2[l]
    return rmsnorm(x, norm_f_g)
```

```python
import functools

import jax
import jax.numpy as jnp
import numpy as np
from jax import lax
from jax.experimental import pallas as pl
from jax.experimental.pallas import tpu as pltpu

D_MODEL = 1024
LRU_WIDTH = 512
LRU_BLOCKS = 8
LRU_BLOCK_DIM = LRU_WIDTH // LRU_BLOCKS
CONV_WIDTH = 4
RG_C = 8.0
RET_HEADS = 8
RET_HEAD_DIM = 64
RET_WIDTH = RET_HEADS * RET_HEAD_DIM
CHUNK = 128
ROPE_BASE = 10000.0
D_FF = 4 * D_MODEL
EPS = 1e-6

SUBLANES = 8
LANES = 128
VMEM_LIMIT_BYTES = 56 * 1024 * 1024

HEAD_PAIRS = RET_WIDTH // LANES
HALF = RET_HEAD_DIM // 2

OFF_XA = 0
OFF_GA = OFF_XA + LRU_WIDTH
OFF_Q = OFF_GA + LRU_WIDTH
OFF_K = OFF_Q + RET_WIDTH
OFF_V = OFF_K + RET_WIDTH
OFF_GR = OFF_V + RET_WIDTH
OFF_SA = OFF_GR + RET_WIDTH
OFF_SB = OFF_SA + D_MODEL
D_IN = OFF_SB + D_MODEL

SEQ_TILE = 256
FFN_TILE = 512
FF_BLOCK = 1024


def _dot(a, b):
    return jnp.dot(a, b, preferred_element_type=jnp.float32)


def _rms(x, g):
    return x * lax.rsqrt(jnp.mean(x * x, axis=-1, keepdims=True) + EPS) * g


def _sigmoid(x):
    return 1.0 / (1.0 + jnp.exp(-x))


def _gelu_tanh(x):
    c = np.float32(np.sqrt(2.0 / np.pi))
    return 0.5 * x * (1.0 + jnp.tanh(c * (x + 0.044715 * (x * x * x))))


def _mixer_kernel(x_ref, g1_ref, win_ref, convw_ref, convb_ref, wg_ref, br_ref, bi_ref,
                  lam_ref, wa_ref, wb_ref, wout_ref, cos_ref, sin_ref, dec_ref, qdec_ref,
                  kdec_ref, cd_ref, smask_ref, o_ref,
                  xa_ext, hcar, state, a_s, u_s, h_s, yb_s):
    ts = x_ref.shape[0]
    groups = ts // SUBLANES

    @pl.when(pl.program_id(1) == 0)
    def _():
        xa_ext[0:SUBLANES, :] = jnp.zeros((SUBLANES, LRU_WIDTH), jnp.float32)
        hcar[...] = jnp.zeros_like(hcar)
        state[...] = jnp.zeros_like(state)

    x = x_ref[...]
    hb = _rms(x, g1_ref[...]).astype(jnp.bfloat16)

    def proj(off, n):
        return _dot(hb, win_ref[:, off:off + n])

    xa = proj(OFF_XA, LRU_WIDTH)
    xa_ext[SUBLANES:SUBLANES + ts, :] = xa
    xc = convb_ref[...] + convw_ref[CONV_WIDTH - 1:CONV_WIDTH, :] * xa
    for j in range(CONV_WIDTH - 1):
        lo = SUBLANES - (CONV_WIDTH - 1) + j
        xc = xc + convw_ref[j:j + 1, :] * xa_ext[lo:lo + ts, :]
    xa_ext[0:SUBLANES, :] = xa[ts - SUBLANES:ts, :]

    xcb = xc.astype(jnp.bfloat16)
    half = LRU_WIDTH // 2
    g0 = _dot(xcb[:, :half], wg_ref[0])
    g1 = _dot(xcb[:, half:], wg_ref[1])
    r = _sigmoid(jnp.concatenate([g0[:, :half], g1[:, :half]], axis=1) + br_ref[...])
    gate_i = _sigmoid(jnp.concatenate([g0[:, half:], g1[:, half:]], axis=1) + bi_ref[...])

    z = -lam_ref[...]
    softplus = jnp.maximum(z, 0.0) + jnp.log1p(jnp.exp(-jnp.abs(z)))
    log_a = (-RG_C) * r * softplus
    a = jnp.exp(log_a)
    mult = jnp.sqrt(-jnp.tanh(log_a) * (a * a + 1.0))
    u = mult * (gate_i * xc)

    a3 = a.reshape(groups, SUBLANES, LRU_WIDTH)
    u3 = u.reshape(groups, SUBLANES, LRU_WIDTH)
    row = lax.broadcasted_iota(jnp.int32, (groups, SUBLANES, LRU_WIDTH), 1)
    shift = 1
    while shift < SUBLANES:
        keep = row >= shift
        a_prev = jnp.where(keep, pltpu.roll(a3, shift, 1), 1.0)
        u_prev = jnp.where(keep, pltpu.roll(u3, shift, 1), 0.0)
        u3 = a3 * u_prev + u3
        a3 = a3 * a_prev
        shift *= 2
    a_s[...] = a3.reshape(ts, LRU_WIDTH)
    u_s[...] = u3.reshape(ts, LRU_WIDTH)

    def carry_step(g, c):
        rows = pl.ds(pl.multiple_of(g * SUBLANES, SUBLANES), SUBLANES)
        hg = u_s[rows, :] + a_s[rows, :] * c
        h_s[rows, :] = hg
        return jnp.broadcast_to(hg[SUBLANES - 1:SUBLANES, :], (SUBLANES, LRU_WIDTH))

    c_out = lax.fori_loop(0, groups, carry_step,
                          jnp.broadcast_to(hcar[...], (SUBLANES, LRU_WIDTH)), unroll=4)
    hcar[...] = c_out[0:1, :]

    ya = (_gelu_tanh(proj(OFF_GA, LRU_WIDTH)) * h_s[...]).astype(jnp.bfloat16)

    lane = lax.broadcasted_iota(jnp.int32, (1, LANES), 1)
    qk_head_a = (lane // HALF) % 2 == 0
    v_head_a = lane < RET_HEAD_DIM
    cosv = cos_ref[...]
    sinv = sin_ref[...]
    for p in range(HEAD_PAIRS):
        lanes_p = slice(p * LANES, (p + 1) * LANES)
        qp = proj(OFF_Q + p * LANES, LANES)
        kp = proj(OFF_K + p * LANES, LANES)
        vp = proj(OFF_V + p * LANES, LANES)
        grp = proj(OFF_GR + p * LANES, LANES)
        qp = qp * cosv + pltpu.roll(qp, LANES // 2, 1) * sinv
        kp = (kp * cosv + pltpu.roll(kp, LANES // 2, 1) * sinv) * (RET_HEAD_DIM ** -0.5)
        for c in range(ts // CHUNK):
            rows = slice(c * CHUNK, (c + 1) * CHUNK)
            qc, kc, vc = qp[rows], kp[rows], vp[rows]
            kcb = kc.astype(jnp.bfloat16)
            vcb = vc.astype(jnp.bfloat16)
            q2 = jnp.concatenate([jnp.where(qk_head_a, qc, 0.0), jnp.where(qk_head_a, 0.0, qc)],
                                 axis=0).astype(jnp.bfloat16)
            scores = lax.dot_general(q2, kcb, (((1,), (1,)), ((), ())),
                                     preferred_element_type=jnp.float32) * dec_ref[p]
            intra2 = _dot(scores.astype(jnp.bfloat16), vcb)
            intra = jnp.where(v_head_a, intra2[:CHUNK], intra2[CHUNK:])
            st = state[p]
            inter = _dot((qc * qdec_ref[p]).astype(jnp.bfloat16), st.astype(jnp.bfloat16))
            kv = lax.dot_general((kc * kdec_ref[p]).astype(jnp.bfloat16), vcb,
                                 (((0,), (0,)), ((), ())), preferred_element_type=jnp.float32)
            state[p] = cd_ref[p] * st + smask_ref[...] * kv
            ret = intra + inter
            sq = ret * ret
            sum_a = jnp.sum(jnp.where(v_head_a, sq, 0.0), axis=-1, keepdims=True)
            sum_b = jnp.sum(jnp.where(v_head_a, 0.0, sq), axis=-1, keepdims=True)
            ms = jnp.where(v_head_a, sum_a, sum_b) * (1.0 / RET_HEAD_DIM)
            gr = grp[rows]
            yb_s[rows, lanes_p] = (gr * _sigmoid(gr)) * (ret * lax.rsqrt(ms + EPS))

    za = _dot(ya, wa_ref[...])
    zb = _dot(yb_s[...].astype(jnp.bfloat16), wb_ref[...])
    m = _sigmoid(proj(OFF_SA, D_MODEL)) * za + _sigmoid(proj(OFF_SB, D_MODEL)) * zb
    o_ref[...] = x + _dot(m.astype(jnp.bfloat16), wout_ref[...])


def _ffn_kernel(x_ref, g2_ref, w1_ref, w2_ref, gf_ref, o_ref, *, final_norm):
    x = x_ref[...]
    hb = _rms(x, g2_ref[...]).astype(jnp.bfloat16)
    acc = x
    for j in range(D_FF // FF_BLOCK):
        cols = slice(j * FF_BLOCK, (j + 1) * FF_BLOCK)
        t = jnp.maximum(_dot(hb, w1_ref[:, cols]), 0.0)
        acc = acc + _dot((t * t).astype(jnp.bfloat16), w2_ref[cols, :])
    o_ref[...] = _rms(acc, gf_ref[...]) if final_norm else acc


def _qk_column_order():
    n = np.arange(RET_WIDTH)
    pair, lane = n // LANES, n % LANES
    block, j = lane // HALF, lane % HALF
    head = 2 * pair + block % 2
    return head * RET_HEAD_DIM + (block // 2) * HALF + j


def _retention_tables(seq):
    nh = RET_HEADS
    log_g = jnp.log1p(-jnp.exp2(-5.0 - jnp.arange(nh, dtype=jnp.float32)))
    pos = jnp.arange(CHUNK, dtype=jnp.float32)
    rel = pos[:, None] - pos[None, :]
    intra = jnp.where(rel[None] >= 0,
                      jnp.exp(log_g[:, None, None] * jnp.maximum(rel, 0.0)[None]), 0.0)
    q_decay = jnp.exp(log_g[:, None] * (pos[None, :] + 1.0))
    k_decay = jnp.exp(log_g[:, None] * (CHUNK - 1.0 - pos[None, :]))
    chunk_decay = jnp.exp(log_g * CHUNK)

    lane = np.arange(LANES)
    qk_sel = (lane // HALF) % 2
    v_sel = lane // RET_HEAD_DIM
    pairs = np.arange(HEAD_PAIRS)
    dec = intra.reshape(HEAD_PAIRS, 2 * CHUNK, CHUNK)
    head_of_lane = 2 * pairs[:, None] + qk_sel[None, :]
    qdec = jnp.transpose(q_decay[head_of_lane], (0, 2, 1))
    kdec = jnp.transpose(k_decay[head_of_lane], (0, 2, 1))
    smask = jnp.asarray(qk_sel[:, None] == v_sel[None, :], jnp.float32)
    cd = chunk_decay[head_of_lane][:, :, None] * smask[None]

    inv = ROPE_BASE ** (-jnp.arange(HALF, dtype=jnp.float32) / HALF)
    ang = jnp.arange(seq, dtype=jnp.float32)[:, None] * inv[None, :]
    cos, sin = jnp.cos(ang), jnp.sin(ang)
    cos_t = jnp.tile(cos, (1, LANES // HALF))
    sin_t = jnp.concatenate([-sin, -sin, sin, sin], axis=1)
    return cos_t, sin_t, dec, qdec, kdec, cd, smask


def _full(shape):
    return pl.BlockSpec(shape, lambda *_: (0,) * len(shape))


def kernel(x, norm1_g, w_in, conv_w, conv_b, lru_wr, lru_br, lru_wi, lru_bi, lru_lambda,
           w_branch_a, w_branch_b, w_out, norm2_g, w_ff1, w_ff2, norm_f_g):
    bsz, seq, _ = x.shape
    depth = w_in.shape[0]
    bf = jnp.bfloat16
    cos_t, sin_t, dec, qdec, kdec, cd, smask = _retention_tables(seq)
    order = _qk_column_order()
    ts = SEQ_TILE
    half = LRU_WIDTH // 2

    for l in range(depth):
        wl = w_in[l]
        win = jnp.concatenate(
            [wl[:, :OFF_Q], wl[:, OFF_Q:OFF_K][:, order], wl[:, OFF_K:OFF_V][:, order], wl[:, OFF_V:]],
            axis=1).astype(bf)
        wr = jax.scipy.linalg.block_diag(*[lru_wr[l, h] for h in range(LRU_BLOCKS)])
        wi = jax.scipy.linalg.block_diag(*[lru_wi[l, h] for h in range(LRU_BLOCKS)])
        wg = jnp.stack([
            jnp.concatenate([wr[:half, :half], wi[:half, :half]], axis=1),
            jnp.concatenate([wr[half:, half:], wi[half:, half:]], axis=1)]).astype(bf)

        x = pl.pallas_call(
            _mixer_kernel,
            name="mixer",
            grid=(bsz, seq // ts),
            in_specs=[
                pl.BlockSpec((None, ts, D_MODEL), lambda b, s: (b, s, 0)),
                _full((1, D_MODEL)),
                _full((D_MODEL, D_IN)),
                _full((CONV_WIDTH, LRU_WIDTH)),
                _full((1, LRU_WIDTH)),
                _full((2, half, LRU_WIDTH)),
                _full((1, LRU_WIDTH)),
                _full((1, LRU_WIDTH)),
                _full((1, LRU_WIDTH)),
                _full((LRU_WIDTH, D_MODEL)),
                _full((RET_WIDTH, D_MODEL)),
                _full((D_MODEL, D_MODEL)),
                pl.BlockSpec((ts, LANES), lambda b, s: (s, 0)),
                pl.BlockSpec((ts, LANES), lambda b, s: (s, 0)),
                _full((HEAD_PAIRS, 2 * CHUNK, CHUNK)),
                _full((HEAD_PAIRS, CHUNK, LANES)),
                _full((HEAD_PAIRS, CHUNK, LANES)),
                _full((HEAD_PAIRS, LANES, LANES)),
                _full((LANES, LANES)),
            ],
            out_specs=pl.BlockSpec((None, ts, D_MODEL), lambda b, s: (b, s, 0)),
            out_shape=jax.ShapeDtypeStruct((bsz, seq, D_MODEL), jnp.float32),
            scratch_shapes=[
                pltpu.VMEM((SUBLANES + ts, LRU_WIDTH), jnp.float32),
                pltpu.VMEM((1, LRU_WIDTH), jnp.float32),
                pltpu.VMEM((HEAD_PAIRS, LANES, LANES), jnp.float32),
                pltpu.VMEM((ts, LRU_WIDTH), jnp.float32),
                pltpu.VMEM((ts, LRU_WIDTH), jnp.float32),
                pltpu.VMEM((ts, LRU_WIDTH), jnp.float32),
                pltpu.VMEM((ts, RET_WIDTH), jnp.float32),
            ],
            compiler_params=pltpu.CompilerParams(
                dimension_semantics=("arbitrary", "arbitrary"),
                vmem_limit_bytes=VMEM_LIMIT_BYTES),
        )(x, norm1_g[l][None], win, conv_w[l], conv_b[l][None], wg,
          lru_br[l].reshape(1, LRU_WIDTH), lru_bi[l].reshape(1, LRU_WIDTH), lru_lambda[l][None],
          w_branch_a[l].astype(bf), w_branch_b[l].astype(bf), w_out[l].astype(bf),
          cos_t, sin_t, dec, qdec, kdec, cd, smask)

        tokens = bsz * seq
        xf = pl.pallas_call(
            functools.partial(_ffn_kernel, final_norm=(l == depth - 1)),
            name="ffn",
            grid=(tokens // FFN_TILE,),
            in_specs=[
                pl.BlockSpec((FFN_TILE, D_MODEL), lambda i: (i, 0)),
                _full((1, D_MODEL)),
                _full((D_MODEL, D_FF)),
                _full((D_FF, D_MODEL)),
                _full((1, D_MODEL)),
            ],
            out_specs=pl.BlockSpec((FFN_TILE, D_MODEL), lambda i: (i, 0)),
            out_shape=jax.ShapeDtypeStruct((tokens, D_MODEL), jnp.float32),
            compiler_params=pltpu.CompilerParams(
                dimension_semantics=("arbitrary",),
                vmem_limit_bytes=VMEM_LIMIT_BYTES),
        )(x.reshape(tokens, D_MODEL), norm2_g[l][None], w_ff1[l].astype(bf), w_ff2[l].astype(bf),
          norm_f_g[None])
        x = xf.reshape(bsz, seq, D_MODEL)
    return x
```

```python
import functools

import jax
import jax.numpy as jnp
import numpy as np
from jax import lax
from jax.experimental import pallas as pl
from jax.experimental.pallas import tpu as pltpu

D_MODEL = 1024
LRU_WIDTH = 512
LRU_BLOCKS = 8
LRU_BLOCK_DIM = LRU_WIDTH // LRU_BLOCKS
CONV_WIDTH = 4
RG_C = 8.0
RET_HEADS = 8
RET_HEAD_DIM = 64
RET_WIDTH = RET_HEADS * RET_HEAD_DIM
CHUNK = 128
ROPE_BASE = 10000.0
D_FF = 4 * D_MODEL
EPS = 1e-6

SUBLANES = 8
LANES = 128
VMEM_LIMIT_BYTES = 56 * 1024 * 1024

HEAD_PAIRS = RET_WIDTH // LANES
HALF = RET_HEAD_DIM // 2

OFF_XA = 0
OFF_GA = OFF_XA + LRU_WIDTH
OFF_Q = OFF_GA + LRU_WIDTH
OFF_K = OFF_Q + RET_WIDTH
OFF_V = OFF_K + RET_WIDTH
OFF_GR = OFF_V + RET_WIDTH
OFF_SA = OFF_GR + RET_WIDTH
OFF_SB = OFF_SA + D_MODEL
D_IN = OFF_SB + D_MODEL

SEQ_TILE = 512
FFN_TILE = 512
FF_BLOCK = 1024


def _dot(a, b):
    return jnp.dot(a, b, preferred_element_type=jnp.float32)


def _rms(x, g):
    return x * lax.rsqrt(jnp.mean(x * x, axis=-1, keepdims=True) + EPS) * g


def _sigmoid(x):
    return 1.0 / (1.0 + jnp.exp(-x))


def _gelu_tanh(x):
    c = np.float32(np.sqrt(2.0 / np.pi))
    return 0.5 * x * (1.0 + jnp.tanh(c * (x + 0.044715 * (x * x * x))))


def _mixer_kernel(x_ref, g1_ref, win_ref, convw_ref, convb_ref, wg_ref, br_ref, bi_ref,
                  lam_ref, wa_ref, wb_ref, wout_ref, cos_ref, sin_ref, dec_ref, qdec_ref,
                  kdec_ref, cd_ref, smask_ref, o_ref,
                  xa_ext, hcar, state, yb_s):
    ts = x_ref.shape[0]
    groups = ts // SUBLANES

    @pl.when(pl.program_id(1) == 0)
    def _():
        xa_ext[0:SUBLANES, :] = jnp.zeros((SUBLANES, LRU_WIDTH), jnp.float32)
        hcar[...] = jnp.zeros_like(hcar)
        state[...] = jnp.zeros_like(state)

    x = x_ref[...]
    hb = _rms(x, g1_ref[...]).astype(jnp.bfloat16)

    def proj(off, n):
        return _dot(hb, win_ref[:, off:off + n])

    xa = proj(OFF_XA, LRU_WIDTH)
    xa_ext[SUBLANES:SUBLANES + ts, :] = xa
    xc = convb_ref[...] + convw_ref[CONV_WIDTH - 1:CONV_WIDTH, :] * xa
    for j in range(CONV_WIDTH - 1):
        lo = SUBLANES - (CONV_WIDTH - 1) + j
        xc = xc + convw_ref[j:j + 1, :] * xa_ext[lo:lo + ts, :]
    xa_ext[0:SUBLANES, :] = xa[ts - SUBLANES:ts, :]

    xcb = xc.astype(jnp.bfloat16)
    half = LRU_WIDTH // 2
    g0 = _dot(xcb[:, :half], wg_ref[0])
    g1 = _dot(xcb[:, half:], wg_ref[1])
    r = _sigmoid(jnp.concatenate([g0[:, :half], g1[:, :half]], axis=1) + br_ref[...])
    gate_i = _sigmoid(jnp.concatenate([g0[:, half:], g1[:, half:]], axis=1) + bi_ref[...])

    z = -lam_ref[...]
    softplus = jnp.maximum(z, 0.0) + jnp.log1p(jnp.exp(-jnp.abs(z)))
    log_a = (-RG_C) * r * softplus
    a = jnp.exp(log_a)
    mult = jnp.sqrt(-jnp.tanh(log_a) * (a * a + 1.0))
    u = mult * (gate_i * xc)

    a3 = a.reshape(groups, SUBLANES, LRU_WIDTH)
    u3 = u.reshape(groups, SUBLANES, LRU_WIDTH)
    row = lax.broadcasted_iota(jnp.int32, (groups, SUBLANES, LRU_WIDTH), 1)
    shift = 1
    while shift < SUBLANES:
        keep = row >= shift
        a_prev = jnp.where(keep, pltpu.roll(a3, shift, 1), 1.0)
        u_prev = jnp.where(keep, pltpu.roll(u3, shift, 1), 0.0)
        u3 = a3 * u_prev + u3
        a3 = a3 * a_prev
        shift *= 2
    a2 = a3.reshape(ts, LRU_WIDTH)
    u2 = u3.reshape(ts, LRU_WIDTH)
    carry = jnp.broadcast_to(hcar[...], (SUBLANES, LRU_WIDTH))
    h_groups = []
    for g in range(groups):
        rows = slice(g * SUBLANES, (g + 1) * SUBLANES)
        hg = u2[rows] + a2[rows] * carry
        h_groups.append(hg)
        carry = jnp.broadcast_to(hg[SUBLANES - 1:SUBLANES, :], (SUBLANES, LRU_WIDTH))
    hcar[...] = carry[0:1, :]
    h = jnp.concatenate(h_groups, axis=0)

    ya = (_gelu_tanh(proj(OFF_GA, LRU_WIDTH)) * h).astype(jnp.bfloat16)

    lane = lax.broadcasted_iota(jnp.int32, (1, LANES), 1)
    qk_head_a = (lane // HALF) % 2 == 0
    v_head_a = lane < RET_HEAD_DIM
    cosv = cos_ref[...]
    sinv = sin_ref[...]
    q_all = proj(OFF_Q, RET_WIDTH)
    k_all = proj(OFF_K, RET_WIDTH)
    v_all = proj(OFF_V, RET_WIDTH)
    gr_all = proj(OFF_GR, RET_WIDTH)
    for p in range(HEAD_PAIRS):
        lanes_p = slice(p * LANES, (p + 1) * LANES)
        qp, kp, vp, grp = q_all[:, lanes_p], k_all[:, lanes_p], v_all[:, lanes_p], gr_all[:, lanes_p]
        qp = qp * cosv + pltpu.roll(qp, LANES // 2, 1) * sinv
        kp = (kp * cosv + pltpu.roll(kp, LANES // 2, 1) * sinv) * (RET_HEAD_DIM ** -0.5)
        for c in range(ts // CHUNK):
            rows = slice(c * CHUNK, (c + 1) * CHUNK)
            qc, kc, vc = qp[rows], kp[rows], vp[rows]
            k2 = jnp.concatenate([jnp.where(qk_head_a, kc, 0.0), jnp.where(qk_head_a, 0.0, kc)],
                                 axis=0).astype(jnp.bfloat16)
            v2 = jnp.concatenate([jnp.where(v_head_a, vc, 0.0), jnp.where(v_head_a, 0.0, vc)],
                                 axis=0).astype(jnp.bfloat16)
            scores = lax.dot_general(qc.astype(jnp.bfloat16), k2, (((1,), (1,)), ((), ())),
                                     preferred_element_type=jnp.float32) * dec_ref[p]
            st = state[p]
            lhs = jnp.concatenate([scores, qc * qdec_ref[p]], axis=1).astype(jnp.bfloat16)
            rhs = jnp.concatenate([v2, st.astype(jnp.bfloat16)], axis=0)
            ret = jnp.dot(lhs, rhs, preferred_element_type=jnp.float32)
            kv = lax.dot_general((kc * kdec_ref[p]).astype(jnp.bfloat16), vc.astype(jnp.bfloat16),
                                 (((0,), (0,)), ((), ())), preferred_element_type=jnp.float32)
            state[p] = cd_ref[p] * st + smask_ref[...] * kv
            sq = ret * ret
            sum_a = jnp.sum(jnp.where(v_head_a, sq, 0.0), axis=-1, keepdims=True)
            sum_b = jnp.sum(jnp.where(v_head_a, 0.0, sq), axis=-1, keepdims=True)
            ms = jnp.where(v_head_a, sum_a, sum_b) * (1.0 / RET_HEAD_DIM)
            gr = grp[rows]
            yb_s[rows, lanes_p] = (gr * _sigmoid(gr)) * (ret * lax.rsqrt(ms + EPS))

    za = _dot(ya, wa_ref[...])
    zb = _dot(yb_s[...].astype(jnp.bfloat16), wb_ref[...])
    m = _sigmoid(proj(OFF_SA, D_MODEL)) * za + _sigmoid(proj(OFF_SB, D_MODEL)) * zb
    o_ref[...] = x + _dot(m.astype(jnp.bfloat16), wout_ref[...])


def _ffn_kernel(x_ref, g2_ref, w1_ref, w2_ref, gf_ref, o_ref, *, final_norm):
    x = x_ref[...]
    hb = _rms(x, g2_ref[...]).astype(jnp.bfloat16)
    acc = x
    for j in range(D_FF // FF_BLOCK):
        cols = slice(j * FF_BLOCK, (j + 1) * FF_BLOCK)
        t = jnp.maximum(_dot(hb, w1_ref[:, cols]), 0.0)
        acc = acc + _dot((t * t).astype(jnp.bfloat16), w2_ref[cols, :])
    o_ref[...] = _rms(acc, gf_ref[...]) if final_norm else acc


def _qk_column_order():
    n = np.arange(RET_WIDTH)
    pair, lane = n // LANES, n % LANES
    block, j = lane // HALF, lane % HALF
    head = 2 * pair + block % 2
    return head * RET_HEAD_DIM + (block // 2) * HALF + j


def _retention_tables(seq):
    nh = RET_HEADS
    log_g = jnp.log1p(-jnp.exp2(-5.0 - jnp.arange(nh, dtype=jnp.float32)))
    pos = jnp.arange(CHUNK, dtype=jnp.float32)
    rel = pos[:, None] - pos[None, :]
    intra = jnp.where(rel[None] >= 0,
                      jnp.exp(log_g[:, None, None] * jnp.maximum(rel, 0.0)[None]), 0.0)
    q_decay = jnp.exp(log_g[:, None] * (pos[None, :] + 1.0))
    k_decay = jnp.exp(log_g[:, None] * (CHUNK - 1.0 - pos[None, :]))
    chunk_decay = jnp.exp(log_g * CHUNK)

    lane = np.arange(LANES)
    qk_sel = (lane // HALF) % 2
    v_sel = lane // RET_HEAD_DIM
    pairs = np.arange(HEAD_PAIRS)
    dec = jnp.concatenate([intra[0::2], intra[1::2]], axis=2)
    head_of_lane = 2 * pairs[:, None] + qk_sel[None, :]
    qdec = jnp.transpose(q_decay[head_of_lane], (0, 2, 1))
    kdec = jnp.transpose(k_decay[head_of_lane], (0, 2, 1))
    smask = jnp.asarray(qk_sel[:, None] == v_sel[None, :], jnp.float32)
    cd = chunk_decay[head_of_lane][:, :, None] * smask[None]

    inv = ROPE_BASE ** (-jnp.arange(HALF, dtype=jnp.float32) / HALF)
    ang = jnp.arange(seq, dtype=jnp.float32)[:, None] * inv[None, :]
    cos, sin = jnp.cos(ang), jnp.sin(ang)
    cos_t = jnp.tile(cos, (1, LANES // HALF))
    sin_t = jnp.concatenate([-sin, -sin, sin, sin], axis=1)
    return cos_t, sin_t, dec, qdec, kdec, cd, smask


def _full(shape):
    return pl.BlockSpec(shape, lambda *_: (0,) * len(shape))


def kernel(x, norm1_g, w_in, conv_w, conv_b, lru_wr, lru_br, lru_wi, lru_bi, lru_lambda,
           w_branch_a, w_branch_b, w_out, norm2_g, w_ff1, w_ff2, norm_f_g):
    bsz, seq, _ = x.shape
    depth = w_in.shape[0]
    bf = jnp.bfloat16
    cos_t, sin_t, dec, qdec, kdec, cd, smask = _retention_tables(seq)
    order = _qk_column_order()
    ts = SEQ_TILE
    half = LRU_WIDTH // 2

    for l in range(depth):
        wl = w_in[l]
        win = jnp.concatenate(
            [wl[:, :OFF_Q], wl[:, OFF_Q:OFF_K][:, order], wl[:, OFF_K:OFF_V][:, order], wl[:, OFF_V:]],
            axis=1).astype(bf)
        wr = jax.scipy.linalg.block_diag(*[lru_wr[l, h] for h in range(LRU_BLOCKS)])
        wi = jax.scipy.linalg.block_diag(*[lru_wi[l, h] for h in range(LRU_BLOCKS)])
        wg = jnp.stack([
            jnp.concatenate([wr[:half, :half], wi[:half, :half]], axis=1),
            jnp.concatenate([wr[half:, half:], wi[half:, half:]], axis=1)]).astype(bf)

        x = pl.pallas_call(
            _mixer_kernel,
            name="mixer",
            grid=(bsz, seq // ts),
            in_specs=[
                pl.BlockSpec((None, ts, D_MODEL), lambda b, s: (b, s, 0)),
                _full((1, D_MODEL)),
                _full((D_MODEL, D_IN)),
                _full((CONV_WIDTH, LRU_WIDTH)),
                _full((1, LRU_WIDTH)),
                _full((2, half, LRU_WIDTH)),
                _full((1, LRU_WIDTH)),
                _full((1, LRU_WIDTH)),
                _full((1, LRU_WIDTH)),
                _full((LRU_WIDTH, D_MODEL)),
                _full((RET_WIDTH, D_MODEL)),
                _full((D_MODEL, D_MODEL)),
                pl.BlockSpec((ts, LANES), lambda b, s: (s, 0)),
                pl.BlockSpec((ts, LANES), lambda b, s: (s, 0)),
                _full((HEAD_PAIRS, CHUNK, 2 * CHUNK)),
                _full((HEAD_PAIRS, CHUNK, LANES)),
                _full((HEAD_PAIRS, CHUNK, LANES)),
                _full((HEAD_PAIRS, LANES, LANES)),
                _full((LANES, LANES)),
            ],
            out_specs=pl.BlockSpec((None, ts, D_MODEL), lambda b, s: (b, s, 0)),
            out_shape=jax.ShapeDtypeStruct((bsz, seq, D_MODEL), jnp.float32),
            scratch_shapes=[
                pltpu.VMEM((SUBLANES + ts, LRU_WIDTH), jnp.float32),
                pltpu.VMEM((1, LRU_WIDTH), jnp.float32),
                pltpu.VMEM((HEAD_PAIRS, LANES, LANES), jnp.float32),
                pltpu.VMEM((ts, RET_WIDTH), jnp.float32),
            ],
            compiler_params=pltpu.CompilerParams(
                dimension_semantics=("arbitrary", "arbitrary"),
                vmem_limit_bytes=VMEM_LIMIT_BYTES),
        )(x, norm1_g[l][None], win, conv_w[l], conv_b[l][None], wg,
          lru_br[l].reshape(1, LRU_WIDTH), lru_bi[l].reshape(1, LRU_WIDTH), lru_lambda[l][None],
          w_branch_a[l].astype(bf), w_branch_b[l].astype(bf), w_out[l].astype(bf),
          cos_t, sin_t, dec, qdec, kdec, cd, smask)

        tokens = bsz * seq
        xf = pl.pallas_call(
            functools.partial(_ffn_kernel, final_norm=(l == depth - 1)),
            name="ffn",
            grid=(tokens // FFN_TILE,),
            in_specs=[
                pl.BlockSpec((FFN_TILE, D_MODEL), lambda i: (i, 0)),
                _full((1, D_MODEL)),
                _full((D_MODEL, D_FF)),
                _full((D_FF, D_MODEL)),
                _full((1, D_MODEL)),
            ],
            out_specs=pl.BlockSpec((FFN_TILE, D_MODEL), lambda i: (i, 0)),
            out_shape=jax.ShapeDtypeStruct((tokens, D_MODEL), jnp.float32),
            compiler_params=pltpu.CompilerParams(
                dimension_semantics=("arbitrary",),
                vmem_limit_bytes=VMEM_LIMIT_BYTES),
        )(x.reshape(tokens, D_MODEL), norm2_g[l][None], w_ff1[l].astype(bf), w_ff2[l].astype(bf),
          norm_f_g[None])
        x = xf.reshape(bsz, seq, D_MODEL)
    return x
```

```python
import functools

import jax
import jax.numpy as jnp
import numpy as np
from jax import lax
from jax.experimental import pallas as pl
from jax.experimental.pallas import tpu as pltpu

D_MODEL = 1024
LRU_WIDTH = 512
LRU_BLOCKS = 8
LRU_BLOCK_DIM = LRU_WIDTH // LRU_BLOCKS
CONV_WIDTH = 4
RG_C = 8.0
RET_HEADS = 8
RET_HEAD_DIM = 64
RET_WIDTH = RET_HEADS * RET_HEAD_DIM
CHUNK = 128
ROPE_BASE = 10000.0
D_FF = 4 * D_MODEL
EPS = 1e-6

SUBLANES = 8
LANES = 128
VMEM_LIMIT_BYTES = 56 * 1024 * 1024

HEAD_PAIRS = RET_WIDTH // LANES
HALF = RET_HEAD_DIM // 2

OFF_XA = 0
OFF_GA = OFF_XA + LRU_WIDTH
OFF_Q = OFF_GA + LRU_WIDTH
OFF_K = OFF_Q + RET_WIDTH
OFF_V = OFF_K + RET_WIDTH
OFF_GR = OFF_V + RET_WIDTH
OFF_SA = OFF_GR + RET_WIDTH
OFF_SB = OFF_SA + D_MODEL
D_IN = OFF_SB + D_MODEL

SEQ_TILE = 512
FFN_TILE = 512
FF_BLOCK = 1024


def _dot(a, b):
    return jnp.dot(a, b, preferred_element_type=jnp.float32)


def _rms(x, g):
    return x * lax.rsqrt(jnp.mean(x * x, axis=-1, keepdims=True) + EPS) * g


def _sigmoid(x):
    return 1.0 / (1.0 + jnp.exp(-x))


def _gelu_tanh(x):
    c = np.float32(np.sqrt(2.0 / np.pi))
    return 0.5 * x * (1.0 + jnp.tanh(c * (x + 0.044715 * (x * x * x))))


def _mixer_kernel(x_ref, g1_ref, win_ref, convw_ref, convb_ref, wg_ref, br_ref, bi_ref,
                  lam_ref, wa_ref, wb_ref, wout_ref, cos_ref, sin_ref, dec_ref, qdec_ref,
                  kdec_ref, cd_ref, smask_ref, o_ref,
                  xa_ext, hcar, state, yb_s):
    ts = x_ref.shape[0]
    groups = ts // SUBLANES

    @pl.when(pl.program_id(1) == 0)
    def _():
        xa_ext[0:SUBLANES, :] = jnp.zeros((SUBLANES, LRU_WIDTH), jnp.float32)
        hcar[...] = jnp.zeros_like(hcar)
        state[...] = jnp.zeros_like(state)

    x = x_ref[...]
    hb = _rms(x, g1_ref[...]).astype(jnp.bfloat16)

    def proj(off, n):
        return _dot(hb, win_ref[:, off:off + n])

    xa = proj(OFF_XA, LRU_WIDTH)
    k_all = proj(OFF_K, RET_WIDTH)
    v_all = proj(OFF_V, RET_WIDTH)
    q_all = proj(OFF_Q, RET_WIDTH)

    xa_ext[SUBLANES:SUBLANES + ts, :] = xa
    xc = convb_ref[...] + convw_ref[CONV_WIDTH - 1:CONV_WIDTH, :] * xa
    for j in range(CONV_WIDTH - 1):
        lo = SUBLANES - (CONV_WIDTH - 1) + j
        xc = xc + convw_ref[j:j + 1, :] * xa_ext[lo:lo + ts, :]
    xa_ext[0:SUBLANES, :] = xa[ts - SUBLANES:ts, :]

    xcb = xc.astype(jnp.bfloat16)
    half = LRU_WIDTH // 2
    g0 = _dot(xcb[:, :half], wg_ref[0])
    g1 = _dot(xcb[:, half:], wg_ref[1])
    gr_all = proj(OFF_GR, RET_WIDTH)
    ga = proj(OFF_GA, LRU_WIDTH)
    sa = proj(OFF_SA, D_MODEL)

    r = _sigmoid(jnp.concatenate([g0[:, :half], g1[:, :half]], axis=1) + br_ref[...])
    gate_i = _sigmoid(jnp.concatenate([g0[:, half:], g1[:, half:]], axis=1) + bi_ref[...])

    z = -lam_ref[...]
    softplus = jnp.maximum(z, 0.0) + jnp.log1p(jnp.exp(-jnp.abs(z)))
    log_a = (-RG_C) * r * softplus
    a = jnp.exp(log_a)
    mult = jnp.sqrt(-jnp.tanh(log_a) * (a * a + 1.0))
    u = mult * (gate_i * xc)

    a3 = a.reshape(groups, SUBLANES, LRU_WIDTH)
    u3 = u.reshape(groups, SUBLANES, LRU_WIDTH)
    row = lax.broadcasted_iota(jnp.int32, (groups, SUBLANES, LRU_WIDTH), 1)
    shift = 1
    while shift < SUBLANES:
        keep = row >= shift
        a_prev = jnp.where(keep, pltpu.roll(a3, shift, 1), 1.0)
        u_prev = jnp.where(keep, pltpu.roll(u3, shift, 1), 0.0)
        u3 = a3 * u_prev + u3
        a3 = a3 * a_prev
        shift *= 2
    a2 = a3.reshape(ts, LRU_WIDTH)
    u2 = u3.reshape(ts, LRU_WIDTH)
    carry = jnp.broadcast_to(hcar[...], (SUBLANES, LRU_WIDTH))
    h_groups = []
    for g in range(groups):
        rows = slice(g * SUBLANES, (g + 1) * SUBLANES)
        hg = u2[rows] + a2[rows] * carry
        h_groups.append(hg)
        carry = jnp.broadcast_to(hg[SUBLANES - 1:SUBLANES, :], (SUBLANES, LRU_WIDTH))
    hcar[...] = carry[0:1, :]
    h = jnp.concatenate(h_groups, axis=0)

    ya = (_gelu_tanh(ga) * h).astype(jnp.bfloat16)

    lane = lax.broadcasted_iota(jnp.int32, (1, LANES), 1)
    qk_head_a = (lane // HALF) % 2 == 0
    v_head_a = lane < RET_HEAD_DIM
    cosv = cos_ref[...]
    sinv = sin_ref[...]
    tasks = [(c, p) for c in range(ts // CHUNK) for p in range(HEAD_PAIRS)]

    def rotate(t_all, p):
        tp = t_all[:, p * LANES:(p + 1) * LANES]
        return tp * cosv + pltpu.roll(tp, LANES // 2, 1) * sinv

    q_rot = [rotate(q_all, p) for p in range(HEAD_PAIRS)]
    k_rot = [rotate(k_all, p) * (RET_HEAD_DIM ** -0.5) for p in range(HEAD_PAIRS)]
    st = [state[p] for p in range(HEAD_PAIRS)]

    def scores_of(c, p):
        rows = slice(c * CHUNK, (c + 1) * CHUNK)
        kc = k_rot[p][rows]
        k2 = jnp.concatenate([jnp.where(qk_head_a, kc, 0.0), jnp.where(qk_head_a, 0.0, kc)],
                             axis=0).astype(jnp.bfloat16)
        return lax.dot_general(q_rot[p][rows].astype(jnp.bfloat16), k2, (((1,), (1,)), ((), ())),
                               preferred_element_type=jnp.float32) * dec_ref[p]

    slice_w = 2 * LANES
    n_slices = D_MODEL // slice_w
    fillers = ([functools.partial(proj, OFF_SB + j * slice_w, slice_w) for j in range(n_slices)]
               + [functools.partial(lambda j: _dot(ya, wa_ref[:, j * slice_w:(j + 1) * slice_w]), j)
                  for j in range(n_slices)])
    filled = []

    scores = scores_of(*tasks[0])
    for i, (c, p) in enumerate(tasks):
        rows = slice(c * CHUNK, (c + 1) * CHUNK)
        lanes_p = slice(p * LANES, (p + 1) * LANES)
        qc, kc, vc = q_rot[p][rows], k_rot[p][rows], v_all[rows, lanes_p]
        v2 = jnp.concatenate([jnp.where(v_head_a, vc, 0.0), jnp.where(v_head_a, 0.0, vc)],
                             axis=0).astype(jnp.bfloat16)
        lhs = jnp.concatenate([scores, qc * qdec_ref[p]], axis=1).astype(jnp.bfloat16)
        rhs = jnp.concatenate([v2, st[p].astype(jnp.bfloat16)], axis=0)
        if i + 1 < len(tasks):
            scores = scores_of(*tasks[i + 1])
        if i % 2 == 0 and fillers:
            filled.append(fillers.pop(0)())
        ret = jnp.dot(lhs, rhs, preferred_element_type=jnp.float32)
        kv = lax.dot_general((kc * kdec_ref[p]).astype(jnp.bfloat16), vc.astype(jnp.bfloat16),
                             (((0,), (0,)), ((), ())), preferred_element_type=jnp.float32)
        st[p] = cd_ref[p] * st[p] + smask_ref[...] * kv
        sq = ret * ret
        sum_a = jnp.sum(jnp.where(v_head_a, sq, 0.0), axis=-1, keepdims=True)
        sum_b = jnp.sum(jnp.where(v_head_a, 0.0, sq), axis=-1, keepdims=True)
        ms = jnp.where(v_head_a, sum_a, sum_b) * (1.0 / RET_HEAD_DIM)
        gr = gr_all[rows, lanes_p]
        yb_s[rows, lanes_p] = (gr * _sigmoid(gr)) * (ret * lax.rsqrt(ms + EPS))
    while fillers:
        filled.append(fillers.pop(0)())
    for p in range(HEAD_PAIRS):
        state[p] = st[p]
    sb = jnp.concatenate(filled[:n_slices], axis=1)
    za = jnp.concatenate(filled[n_slices:], axis=1)

    zb = _dot(yb_s[...].astype(jnp.bfloat16), wb_ref[...])
    m = _sigmoid(sa) * za + _sigmoid(sb) * zb
    o_ref[...] = x + _dot(m.astype(jnp.bfloat16), wout_ref[...])


def _ffn_kernel(x_ref, g2_ref, w1_ref, w2_ref, gf_ref, o_ref, *, final_norm):
    x = x_ref[...]
    hb = _rms(x, g2_ref[...]).astype(jnp.bfloat16)
    acc = x
    for j in range(D_FF // FF_BLOCK):
        cols = slice(j * FF_BLOCK, (j + 1) * FF_BLOCK)
        t = jnp.maximum(_dot(hb, w1_ref[:, cols]), 0.0)
        acc = acc + _dot((t * t).astype(jnp.bfloat16), w2_ref[cols, :])
    o_ref[...] = _rms(acc, gf_ref[...]) if final_norm else acc


def _qk_column_order():
    n = np.arange(RET_WIDTH)
    pair, lane = n // LANES, n % LANES
    block, j = lane // HALF, lane % HALF
    head = 2 * pair + block % 2
    return head * RET_HEAD_DIM + (block // 2) * HALF + j


def _retention_tables(seq):
    nh = RET_HEADS
    log_g = jnp.log1p(-jnp.exp2(-5.0 - jnp.arange(nh, dtype=jnp.float32)))
    pos = jnp.arange(CHUNK, dtype=jnp.float32)
    rel = pos[:, None] - pos[None, :]
    intra = jnp.where(rel[None] >= 0,
                      jnp.exp(log_g[:, None, None] * jnp.maximum(rel, 0.0)[None]), 0.0)
    q_decay = jnp.exp(log_g[:, None] * (pos[None, :] + 1.0))
    k_decay = jnp.exp(log_g[:, None] * (CHUNK - 1.0 - pos[None, :]))
    chunk_decay = jnp.exp(log_g * CHUNK)

    lane = np.arange(LANES)
    qk_sel = (lane // HALF) % 2
    v_sel = lane // RET_HEAD_DIM
    pairs = np.arange(HEAD_PAIRS)
    dec = jnp.concatenate([intra[0::2], intra[1::2]], axis=2)
    head_of_lane = 2 * pairs[:, None] + qk_sel[None, :]
    qdec = jnp.transpose(q_decay[head_of_lane], (0, 2, 1))
    kdec = jnp.transpose(k_decay[head_of_lane], (0, 2, 1))
    smask = jnp.asarray(qk_sel[:, None] == v_sel[None, :], jnp.float32)
    cd = chunk_decay[head_of_lane][:, :, None] * smask[None]

    inv = ROPE_BASE ** (-jnp.arange(HALF, dtype=jnp.float32) / HALF)
    ang = jnp.arange(seq, dtype=jnp.float32)[:, None] * inv[None, :]
    cos, sin = jnp.cos(ang), jnp.sin(ang)
    cos_t = jnp.tile(cos, (1, LANES // HALF))
    sin_t = jnp.concatenate([-sin, -sin, sin, sin], axis=1)
    return cos_t, sin_t, dec, qdec, kdec, cd, smask


def _full(shape):
    return pl.BlockSpec(shape, lambda *_: (0,) * len(shape))


def kernel(x, norm1_g, w_in, conv_w, conv_b, lru_wr, lru_br, lru_wi, lru_bi, lru_lambda,
           w_branch_a, w_branch_b, w_out, norm2_g, w_ff1, w_ff2, norm_f_g):
    bsz, seq, _ = x.shape
    depth = w_in.shape[0]
    bf = jnp.bfloat16
    cos_t, sin_t, dec, qdec, kdec, cd, smask = _retention_tables(seq)
    order = _qk_column_order()
    ts = SEQ_TILE
    half = LRU_WIDTH // 2

    for l in range(depth):
        wl = w_in[l]
        win = jnp.concatenate(
            [wl[:, :OFF_Q], wl[:, OFF_Q:OFF_K][:, order], wl[:, OFF_K:OFF_V][:, order], wl[:, OFF_V:]],
            axis=1).astype(bf)
        wr = jax.scipy.linalg.block_diag(*[lru_wr[l, h] for h in range(LRU_BLOCKS)])
        wi = jax.scipy.linalg.block_diag(*[lru_wi[l, h] for h in range(LRU_BLOCKS)])
        wg = jnp.stack([
            jnp.concatenate([wr[:half, :half], wi[:half, :half]], axis=1),
            jnp.concatenate([wr[half:, half:], wi[half:, half:]], axis=1)]).astype(bf)

        x = pl.pallas_call(
            _mixer_kernel,
            name="mixer",
            grid=(bsz, seq // ts),
            in_specs=[
                pl.BlockSpec((None, ts, D_MODEL), lambda b, s: (b, s, 0)),
                _full((1, D_MODEL)),
                _full((D_MODEL, D_IN)),
                _full((CONV_WIDTH, LRU_WIDTH)),
                _full((1, LRU_WIDTH)),
                _full((2, half, LRU_WIDTH)),
                _full((1, LRU_WIDTH)),
                _full((1, LRU_WIDTH)),
                _full((1, LRU_WIDTH)),
                _full((LRU_WIDTH, D_MODEL)),
                _full((RET_WIDTH, D_MODEL)),
                _full((D_MODEL, D_MODEL)),
                pl.BlockSpec((ts, LANES), lambda b, s: (s, 0)),
                pl.BlockSpec((ts, LANES), lambda b, s: (s, 0)),
                _full((HEAD_PAIRS, CHUNK, 2 * CHUNK)),
                _full((HEAD_PAIRS, CHUNK, LANES)),
                _full((HEAD_PAIRS, CHUNK, LANES)),
                _full((HEAD_PAIRS, LANES, LANES)),
                _full((LANES, LANES)),
            ],
            out_specs=pl.BlockSpec((None, ts, D_MODEL), lambda b, s: (b, s, 0)),
            out_shape=jax.ShapeDtypeStruct((bsz, seq, D_MODEL), jnp.float32),
            scratch_shapes=[
                pltpu.VMEM((SUBLANES + ts, LRU_WIDTH), jnp.float32),
                pltpu.VMEM((1, LRU_WIDTH), jnp.float32),
                pltpu.VMEM((HEAD_PAIRS, LANES, LANES), jnp.float32),
                pltpu.VMEM((ts, RET_WIDTH), jnp.float32),
            ],
            compiler_params=pltpu.CompilerParams(
                dimension_semantics=("arbitrary", "arbitrary"),
                vmem_limit_bytes=VMEM_LIMIT_BYTES),
        )(x, norm1_g[l][None], win, conv_w[l], conv_b[l][None], wg,
          lru_br[l].reshape(1, LRU_WIDTH), lru_bi[l].reshape(1, LRU_WIDTH), lru_lambda[l][None],
          w_branch_a[l].astype(bf), w_branch_b[l].astype(bf), w_out[l].astype(bf),
          cos_t, sin_t, dec, qdec, kdec, cd, smask)

        tokens = bsz * seq
        xf = pl.pallas_call(
            functools.partial(_ffn_kernel, final_norm=(l == depth - 1)),
            name="ffn",
            grid=(tokens // FFN_TILE,),
            in_specs=[
                pl.BlockSpec((FFN_TILE, D_MODEL), lambda i: (i, 0)),
                _full((1, D_MODEL)),
                _full((D_MODEL, D_FF)),
                _full((D_FF, D_MODEL)),
                _full((1, D_MODEL)),
            ],
            out_specs=pl.BlockSpec((FFN_TILE, D_MODEL), lambda i: (i, 0)),
            out_shape=jax.ShapeDtypeStruct((tokens, D_MODEL), jnp.float32),
            compiler_params=pltpu.CompilerParams(
                dimension_semantics=("arbitrary",),
                vmem_limit_bytes=VMEM_LIMIT_BYTES),
        )(x.reshape(tokens, D_MODEL), norm2_g[l][None], w_ff1[l].astype(bf), w_ff2[l].astype(bf),
          norm_f_g[None])
        x = xf.reshape(bsz, seq, D_MODEL)
    return x
```

```python
import functools

import jax
import jax.numpy as jnp
import numpy as np
from jax import lax
from jax.experimental import pallas as pl
from jax.experimental.pallas import tpu as pltpu

D_MODEL = 1024
LRU_WIDTH = 512
LRU_BLOCKS = 8
LRU_BLOCK_DIM = LRU_WIDTH // LRU_BLOCKS
CONV_WIDTH = 4
RG_C = 8.0
RET_HEADS = 8
RET_HEAD_DIM = 64
RET_WIDTH = RET_HEADS * RET_HEAD_DIM
CHUNK = 128
ROPE_BASE = 10000.0
D_FF = 4 * D_MODEL
EPS = 1e-6

SUBLANES = 8
LANES = 128
VMEM_LIMIT_BYTES = 56 * 1024 * 1024

HEAD_PAIRS = RET_WIDTH // LANES
HALF = RET_HEAD_DIM // 2

OFF_XA = 0
OFF_GA = OFF_XA + LRU_WIDTH
OFF_Q = OFF_GA + LRU_WIDTH
OFF_K = OFF_Q + RET_WIDTH
OFF_V = OFF_K + RET_WIDTH
OFF_GR = OFF_V + RET_WIDTH
OFF_SA = OFF_GR + RET_WIDTH
OFF_SB = OFF_SA + D_MODEL
D_IN = OFF_SB + D_MODEL

SEQ_TILE = 512
FFN_TILE = 512
FF_BLOCK = 1024


def _dot(a, b):
    return jnp.dot(a, b, preferred_element_type=jnp.float32)


def _rms(x, g):
    return x * lax.rsqrt(jnp.mean(x * x, axis=-1, keepdims=True) + EPS) * g


def _sigmoid(x):
    return 1.0 / (1.0 + jnp.exp(-x))


def _gelu_tanh(x):
    c = np.float32(np.sqrt(2.0 / np.pi))
    return 0.5 * x * (1.0 + jnp.tanh(c * (x + 0.044715 * (x * x * x))))


def _mixer_kernel(x_ref, g1_ref, win_ref, convw_ref, convb_ref, wg_ref, br_ref, bi_ref,
                  lam_ref, wa_ref, wb_ref, wout_ref, cos_ref, sin_ref, dec_ref, qdec_ref,
                  kdec_ref, cd_ref, smask_ref, o_ref,
                  xa_ext, hcar, state, yb_s):
    ts = x_ref.shape[0]
    groups = ts // SUBLANES

    @pl.when(pl.program_id(1) == 0)
    def _():
        xa_ext[0:SUBLANES, :] = jnp.zeros((SUBLANES, LRU_WIDTH), jnp.float32)
        hcar[...] = jnp.zeros_like(hcar)
        state[...] = jnp.zeros_like(state)

    x = x_ref[...]
    hb = _rms(x, g1_ref[...]).astype(jnp.bfloat16)

    def proj(off, n):
        return _dot(hb, win_ref[:, off:off + n])

    xa = proj(OFF_XA, LRU_WIDTH)
    k_all = proj(OFF_K, RET_WIDTH)
    v_all = proj(OFF_V, RET_WIDTH)
    q_all = proj(OFF_Q, RET_WIDTH)

    xa_ext[SUBLANES:SUBLANES + ts, :] = xa
    xc = convb_ref[...] + convw_ref[CONV_WIDTH - 1:CONV_WIDTH, :] * xa
    for j in range(CONV_WIDTH - 1):
        lo = SUBLANES - (CONV_WIDTH - 1) + j
        xc = xc + convw_ref[j:j + 1, :] * xa_ext[lo:lo + ts, :]
    xa_ext[0:SUBLANES, :] = xa[ts - SUBLANES:ts, :]

    xcb = xc.astype(jnp.bfloat16)
    half = LRU_WIDTH // 2
    g0 = _dot(xcb[:, :half], wg_ref[0])
    g1 = _dot(xcb[:, half:], wg_ref[1])
    gr_all = proj(OFF_GR, RET_WIDTH)
    ga = proj(OFF_GA, LRU_WIDTH)
    sa = proj(OFF_SA, D_MODEL)

    r = _sigmoid(jnp.concatenate([g0[:, :half], g1[:, :half]], axis=1) + br_ref[...])
    gate_i = _sigmoid(jnp.concatenate([g0[:, half:], g1[:, half:]], axis=1) + bi_ref[...])

    z = -lam_ref[...]
    softplus = jnp.maximum(z, 0.0) + jnp.log1p(jnp.exp(-jnp.abs(z)))
    log_a = (-RG_C) * r * softplus
    a = jnp.exp(log_a)
    mult = jnp.sqrt(-jnp.tanh(log_a) * (a * a + 1.0))
    u = mult * (gate_i * xc)

    a3 = a.reshape(groups, SUBLANES, LRU_WIDTH)
    u3 = u.reshape(groups, SUBLANES, LRU_WIDTH)
    row = lax.broadcasted_iota(jnp.int32, (groups, SUBLANES, LRU_WIDTH), 1)
    shift = 1
    while shift < SUBLANES:
        keep = row >= shift
        a_prev = jnp.where(keep, pltpu.roll(a3, shift, 1), 1.0)
        u_prev = jnp.where(keep, pltpu.roll(u3, shift, 1), 0.0)
        u3 = a3 * u_prev + u3
        a3 = a3 * a_prev
        shift *= 2
    a2 = a3.reshape(ts, LRU_WIDTH)
    u2 = u3.reshape(ts, LRU_WIDTH)
    carry = jnp.broadcast_to(hcar[...], (SUBLANES, LRU_WIDTH))
    h_groups = []
    for g in range(groups):
        rows = slice(g * SUBLANES, (g + 1) * SUBLANES)
        hg = u2[rows] + a2[rows] * carry
        h_groups.append(hg)
        carry = jnp.broadcast_to(hg[SUBLANES - 1:SUBLANES, :], (SUBLANES, LRU_WIDTH))
    hcar[...] = carry[0:1, :]
    h = jnp.concatenate(h_groups, axis=0)

    ya = (_gelu_tanh(ga) * h).astype(jnp.bfloat16)

    lane = lax.broadcasted_iota(jnp.int32, (1, LANES), 1)
    qk_head_a = (lane // HALF) % 2 == 0
    v_head_a = lane < RET_HEAD_DIM
    cosv = cos_ref[...]
    sinv = sin_ref[...]
    tasks = [(c, p) for c in range(ts // CHUNK) for p in range(HEAD_PAIRS)]

    def rotate(t_all, p):
        tp = t_all[:, p * LANES:(p + 1) * LANES]
        return tp * cosv + pltpu.roll(tp, LANES // 2, 1) * sinv

    q_rot = [rotate(q_all, p) for p in range(HEAD_PAIRS)]
    k_rot = [rotate(k_all, p) * (RET_HEAD_DIM ** -0.5) for p in range(HEAD_PAIRS)]
    st = [state[p] for p in range(HEAD_PAIRS)]

    def scores_of(c, p):
        rows = slice(c * CHUNK, (c + 1) * CHUNK)
        kc = k_rot[p][rows]
        k2 = jnp.concatenate([jnp.where(qk_head_a, kc, 0.0), jnp.where(qk_head_a, 0.0, kc)],
                             axis=0).astype(jnp.bfloat16)
        return lax.dot_general(q_rot[p][rows].astype(jnp.bfloat16), k2, (((1,), (1,)), ((), ())),
                               preferred_element_type=jnp.float32) * dec_ref[p]

    slice_w = 2 * LANES
    n_slices = D_MODEL // slice_w
    fillers = ([functools.partial(proj, OFF_SB + j * slice_w, slice_w) for j in range(n_slices)]
               + [functools.partial(lambda j: _dot(ya, wa_ref[:, j * slice_w:(j + 1) * slice_w]), j)
                  for j in range(n_slices)])
    filled = []

    scores = scores_of(*tasks[0])
    for i, (c, p) in enumerate(tasks):
        rows = slice(c * CHUNK, (c + 1) * CHUNK)
        lanes_p = slice(p * LANES, (p + 1) * LANES)
        qc, kc, vc = q_rot[p][rows], k_rot[p][rows], v_all[rows, lanes_p]
        v2 = jnp.concatenate([jnp.where(v_head_a, vc, 0.0), jnp.where(v_head_a, 0.0, vc)],
                             axis=0).astype(jnp.bfloat16)
        lhs = jnp.concatenate([scores, qc * qdec_ref[p]], axis=1).astype(jnp.bfloat16)
        rhs = jnp.concatenate([v2, st[p].astype(jnp.bfloat16)], axis=0)
        if i + 1 < len(tasks):
            scores = scores_of(*tasks[i + 1])
        if i % 2 == 0 and fillers:
            filled.append(fillers.pop(0)())
        ret = jnp.dot(lhs, rhs, preferred_element_type=jnp.float32)
        kv = lax.dot_general((kc * kdec_ref[p]).astype(jnp.bfloat16), vc.astype(jnp.bfloat16),
                             (((0,), (0,)), ((), ())), preferred_element_type=jnp.float32)
        st[p] = cd_ref[p] * st[p] + smask_ref[...] * kv
        sq = ret * ret
        sum_a = jnp.sum(jnp.where(v_head_a, sq, 0.0), axis=-1, keepdims=True)
        sum_b = jnp.sum(jnp.where(v_head_a, 0.0, sq), axis=-1, keepdims=True)
        ms = jnp.where(v_head_a, sum_a, sum_b) * (1.0 / RET_HEAD_DIM)
        gr = gr_all[rows, lanes_p]
        yb_s[rows, lanes_p] = (gr * _sigmoid(gr)) * (ret * lax.rsqrt(ms + EPS))
    while fillers:
        filled.append(fillers.pop(0)())
    for p in range(HEAD_PAIRS):
        state[p] = st[p]
    sb = jnp.concatenate(filled[:n_slices], axis=1)
    za = jnp.concatenate(filled[n_slices:], axis=1)

    zb = _dot(yb_s[...].astype(jnp.bfloat16), wb_ref[...])
    m = _sigmoid(sa) * za + _sigmoid(sb) * zb
    o_ref[...] = x + _dot(m.astype(jnp.bfloat16), wout_ref[...])


def _ffn_kernel(x_ref, g2_ref, w1_ref, w2_ref, gf_ref, o_ref, *, final_norm):
    x = x_ref[...]
    hb = _rms(x, g2_ref[...]).astype(jnp.bfloat16)
    acc = x
    for j in range(D_FF // FF_BLOCK):
        cols = slice(j * FF_BLOCK, (j + 1) * FF_BLOCK)
        t = jnp.maximum(_dot(hb, w1_ref[:, cols]), 0.0)
        acc = acc + _dot((t * t).astype(jnp.bfloat16), w2_ref[cols, :])
    o_ref[...] = _rms(acc, gf_ref[...]) if final_norm else acc


def _qk_column_order():
    n = np.arange(RET_WIDTH)
    pair, lane = n // LANES, n % LANES
    block, j = lane // HALF, lane % HALF
    head = 2 * pair + block % 2
    return head * RET_HEAD_DIM + (block // 2) * HALF + j


def _retention_tables(seq):
    log_g = np.log1p(-np.exp2(-5.0 - np.arange(RET_HEADS, dtype=np.float64)))
    pos = np.arange(CHUNK, dtype=np.float64)
    rel = pos[:, None] - pos[None, :]
    intra = np.where(rel[None] >= 0, np.exp(log_g[:, None, None] * np.maximum(rel, 0.0)[None]), 0.0)
    q_decay = np.exp(log_g[:, None] * (pos[None, :] + 1.0))
    k_decay = np.exp(log_g[:, None] * (CHUNK - 1.0 - pos[None, :]))
    chunk_decay = np.exp(log_g * CHUNK)

    lane = np.arange(LANES)
    qk_sel = (lane // HALF) % 2
    v_sel = lane // RET_HEAD_DIM
    pairs = np.arange(HEAD_PAIRS)
    dec = np.concatenate([intra[0::2], intra[1::2]], axis=2)
    head_of_lane = 2 * pairs[:, None] + qk_sel[None, :]
    qdec = np.transpose(q_decay[head_of_lane], (0, 2, 1))
    kdec = np.transpose(k_decay[head_of_lane], (0, 2, 1))
    smask = (qk_sel[:, None] == v_sel[None, :]).astype(np.float64)
    cd = chunk_decay[head_of_lane][:, :, None] * smask[None]

    inv = ROPE_BASE ** (-np.arange(HALF, dtype=np.float64) / HALF)
    ang = np.arange(seq, dtype=np.float64)[:, None] * inv[None, :]
    cos, sin = np.cos(ang), np.sin(ang)
    cos_t = np.tile(cos, (1, LANES // HALF))
    sin_t = np.concatenate([-sin, -sin, sin, sin], axis=1)
    return tuple(jnp.asarray(t, jnp.float32) for t in (cos_t, sin_t, dec, qdec, kdec, cd, smask))


def _gate_weights(w):
    per_half = LRU_BLOCKS // 2
    rows = w.reshape(2, per_half * LRU_BLOCK_DIM, LRU_BLOCK_DIM)
    tiled = jnp.tile(rows, (1, 1, per_half))
    blk = np.arange(per_half * LRU_BLOCK_DIM) // LRU_BLOCK_DIM
    return tiled * jnp.asarray(blk[:, None] == blk[None, :], w.dtype)


def _full(shape):
    return pl.BlockSpec(shape, lambda *_: (0,) * len(shape))


def kernel(x, norm1_g, w_in, conv_w, conv_b, lru_wr, lru_br, lru_wi, lru_bi, lru_lambda,
           w_branch_a, w_branch_b, w_out, norm2_g, w_ff1, w_ff2, norm_f_g):
    bsz, seq, _ = x.shape
    depth = w_in.shape[0]
    bf = jnp.bfloat16
    cos_t, sin_t, dec, qdec, kdec, cd, smask = _retention_tables(seq)
    order = _qk_column_order()
    columns = np.concatenate([np.arange(OFF_Q), OFF_Q + order, OFF_K + order, np.arange(OFF_V, D_IN)])
    ts = SEQ_TILE
    half = LRU_WIDTH // 2

    def row_param(p):
        return p.reshape(depth, 1, -1)

    def layer_row(l, width):
        return pl.BlockSpec((None, 1, width), lambda *_: (l, 0, 0))

    for l in range(depth):
        win = jnp.take(w_in[l], columns, axis=1).astype(bf)
        wg = jnp.concatenate([_gate_weights(lru_wr[l]), _gate_weights(lru_wi[l])], axis=2).astype(bf)

        x = pl.pallas_call(
            _mixer_kernel,
            name="mixer",
            grid=(bsz, seq // ts),
            in_specs=[
                pl.BlockSpec((None, ts, D_MODEL), lambda b, s: (b, s, 0)),
                layer_row(l, D_MODEL),
                _full((D_MODEL, D_IN)),
                pl.BlockSpec((None, CONV_WIDTH, LRU_WIDTH), lambda *_: (l, 0, 0)),
                layer_row(l, LRU_WIDTH),
                _full((2, half, LRU_WIDTH)),
                layer_row(l, LRU_WIDTH),
                layer_row(l, LRU_WIDTH),
                layer_row(l, LRU_WIDTH),
                _full((LRU_WIDTH, D_MODEL)),
                _full((RET_WIDTH, D_MODEL)),
                _full((D_MODEL, D_MODEL)),
                pl.BlockSpec((ts, LANES), lambda b, s: (s, 0)),
                pl.BlockSpec((ts, LANES), lambda b, s: (s, 0)),
                _full((HEAD_PAIRS, CHUNK, 2 * CHUNK)),
                _full((HEAD_PAIRS, CHUNK, LANES)),
                _full((HEAD_PAIRS, CHUNK, LANES)),
                _full((HEAD_PAIRS, LANES, LANES)),
                _full((LANES, LANES)),
            ],
            out_specs=pl.BlockSpec((None, ts, D_MODEL), lambda b, s: (b, s, 0)),
            out_shape=jax.ShapeDtypeStruct((bsz, seq, D_MODEL), jnp.float32),
            scratch_shapes=[
                pltpu.VMEM((SUBLANES + ts, LRU_WIDTH), jnp.float32),
                pltpu.VMEM((1, LRU_WIDTH), jnp.float32),
                pltpu.VMEM((HEAD_PAIRS, LANES, LANES), jnp.float32),
                pltpu.VMEM((ts, RET_WIDTH), jnp.float32),
            ],
            compiler_params=pltpu.CompilerParams(
                dimension_semantics=("arbitrary", "arbitrary"),
                vmem_limit_bytes=VMEM_LIMIT_BYTES),
        )(x, row_param(norm1_g), win, conv_w, row_param(conv_b), wg,
          row_param(lru_br), row_param(lru_bi), row_param(lru_lambda),
          w_branch_a[l].astype(bf), w_branch_b[l].astype(bf), w_out[l].astype(bf),
          cos_t, sin_t, dec, qdec, kdec, cd, smask)

        tokens = bsz * seq
        xf = pl.pallas_call(
            functools.partial(_ffn_kernel, final_norm=(l == depth - 1)),
            name="ffn",
            grid=(tokens // FFN_TILE,),
            in_specs=[
                pl.BlockSpec((FFN_TILE, D_MODEL), lambda i: (i, 0)),
                layer_row(l, D_MODEL),
                _full((D_MODEL, D_FF)),
                _full((D_FF, D_MODEL)),
                _full((1, D_MODEL)),
            ],
            out_specs=pl.BlockSpec((FFN_TILE, D_MODEL), lambda i: (i, 0)),
            out_shape=jax.ShapeDtypeStruct((tokens, D_MODEL), jnp.float32),
            compiler_params=pltpu.CompilerParams(
                dimension_semantics=("arbitrary",),
                vmem_limit_bytes=VMEM_LIMIT_BYTES),
        )(x.reshape(tokens, D_MODEL), row_param(norm2_g), w_ff1[l].astype(bf), w_ff2[l].astype(bf),
          norm_f_g[None])
        x = xf.reshape(bsz, seq, D_MODEL)
    return x
```

```python
import functools

import jax
import jax.numpy as jnp
import numpy as np
from jax import lax
from jax.experimental import pallas as pl
from jax.experimental.pallas import tpu as pltpu

D_MODEL = 1024
LRU_WIDTH = 512
LRU_BLOCKS = 8
LRU_BLOCK_DIM = LRU_WIDTH // LRU_BLOCKS
CONV_WIDTH = 4
RG_C = 8.0
RET_HEADS = 8
RET_HEAD_DIM = 64
RET_WIDTH = RET_HEADS * RET_HEAD_DIM
CHUNK = 128
ROPE_BASE = 10000.0
D_FF = 4 * D_MODEL
EPS = 1e-6

SUBLANES = 8
LANES = 128
VMEM_LIMIT_BYTES = 56 * 1024 * 1024

HEAD_PAIRS = RET_WIDTH // LANES
HALF = RET_HEAD_DIM // 2

OFF_XA = 0
OFF_GA = OFF_XA + LRU_WIDTH
OFF_Q = OFF_GA + LRU_WIDTH
OFF_K = OFF_Q + RET_WIDTH
OFF_V = OFF_K + RET_WIDTH
OFF_GR = OFF_V + RET_WIDTH
OFF_SA = OFF_GR + RET_WIDTH
OFF_SB = OFF_SA + D_MODEL
D_IN = OFF_SB + D_MODEL

SEQ_TILE = 512
FFN_TILE = 512
FF_BLOCK = 1024


def _dot(a, b):
    return jnp.dot(a, b, preferred_element_type=jnp.float32)


def _rms(x, g):
    return x * lax.rsqrt(jnp.mean(x * x, axis=-1, keepdims=True) + EPS) * g


def _sigmoid(x):
    return 1.0 / (1.0 + jnp.exp(-x))


def _gelu_tanh(x):
    c = np.float32(np.sqrt(2.0 / np.pi))
    return 0.5 * x * (1.0 + jnp.tanh(c * (x + 0.044715 * (x * x * x))))


def _mixer_kernel(x_ref, g1_ref, win_ref, convw_ref, convb_ref, wg_ref, br_ref, bi_ref,
                  lam_ref, wa_ref, wb_ref, wout_ref, cos_ref, sin_ref, dec_ref, qdec_ref,
                  kdec_ref, cd_ref, smask_ref, o_ref,
                  xa_ext, hcar, state, yb_s):
    ts = x_ref.shape[0]
    groups = ts // SUBLANES

    @pl.when(pl.program_id(1) == 0)
    def _():
        xa_ext[0:SUBLANES, :] = jnp.zeros((SUBLANES, LRU_WIDTH), jnp.float32)
        hcar[...] = jnp.zeros_like(hcar)
        state[...] = jnp.zeros_like(state)

    x = x_ref[...]
    hb = _rms(x, g1_ref[...]).astype(jnp.bfloat16)

    def proj(off, n):
        return _dot(hb, win_ref[:, off:off + n])

    xa = proj(OFF_XA, LRU_WIDTH)
    k_all = proj(OFF_K, RET_WIDTH)
    v_all = proj(OFF_V, RET_WIDTH)
    q_all = proj(OFF_Q, RET_WIDTH)

    xa_ext[SUBLANES:SUBLANES + ts, :] = xa
    xc = convb_ref[...] + convw_ref[CONV_WIDTH - 1:CONV_WIDTH, :] * xa
    for j in range(CONV_WIDTH - 1):
        lo = SUBLANES - (CONV_WIDTH - 1) + j
        xc = xc + convw_ref[j:j + 1, :] * xa_ext[lo:lo + ts, :]
    xa_ext[0:SUBLANES, :] = xa[ts - SUBLANES:ts, :]

    xcb = xc.astype(jnp.bfloat16)
    half = LRU_WIDTH // 2
    g0 = _dot(xcb[:, :half], wg_ref[0])
    g1 = _dot(xcb[:, half:], wg_ref[1])
    gr_all = proj(OFF_GR, RET_WIDTH)
    ga = proj(OFF_GA, LRU_WIDTH)
    sa = proj(OFF_SA, D_MODEL)

    r = _sigmoid(jnp.concatenate([g0[:, :half], g1[:, :half]], axis=1) + br_ref[...])
    gate_i = _sigmoid(jnp.concatenate([g0[:, half:], g1[:, half:]], axis=1) + bi_ref[...])

    z = -lam_ref[...]
    softplus = jnp.maximum(z, 0.0) + jnp.log1p(jnp.exp(-jnp.abs(z)))
    log_a = (-RG_C) * r * softplus
    a = jnp.exp(log_a)
    mult = jnp.sqrt(-jnp.tanh(log_a) * (a * a + 1.0))
    u = mult * (gate_i * xc)

    a3 = a.reshape(groups, SUBLANES, LRU_WIDTH)
    u3 = u.reshape(groups, SUBLANES, LRU_WIDTH)
    row = lax.broadcasted_iota(jnp.int32, (groups, SUBLANES, LRU_WIDTH), 1)
    shift = 1
    while shift < SUBLANES:
        keep = row >= shift
        a_prev = jnp.where(keep, pltpu.roll(a3, shift, 1), 1.0)
        u_prev = jnp.where(keep, pltpu.roll(u3, shift, 1), 0.0)
        u3 = a3 * u_prev + u3
        a3 = a3 * a_prev
        shift *= 2
    a2 = a3.reshape(ts, LRU_WIDTH)
    u2 = u3.reshape(ts, LRU_WIDTH)
    carry = jnp.broadcast_to(hcar[...], (SUBLANES, LRU_WIDTH))
    h_groups = []
    for g in range(groups):
        rows = slice(g * SUBLANES, (g + 1) * SUBLANES)
        hg = u2[rows] + a2[rows] * carry
        h_groups.append(hg)
        carry = jnp.broadcast_to(hg[SUBLANES - 1:SUBLANES, :], (SUBLANES, LRU_WIDTH))
    hcar[...] = carry[0:1, :]
    h = jnp.concatenate(h_groups, axis=0)

    ya = (_gelu_tanh(ga) * h).astype(jnp.bfloat16)

    lane = lax.broadcasted_iota(jnp.int32, (1, LANES), 1)
    qk_head_a = (lane // HALF) % 2 == 0
    v_head_a = lane < RET_HEAD_DIM
    cosv = cos_ref[...]
    sinv = sin_ref[...]
    tasks = [(c, p) for c in range(ts // CHUNK) for p in range(HEAD_PAIRS)]

    def rotate(t_all, p):
        tp = t_all[:, p * LANES:(p + 1) * LANES]
        return tp * cosv + pltpu.roll(tp, LANES // 2, 1) * sinv

    q_rot = [rotate(q_all, p) for p in range(HEAD_PAIRS)]
    k_rot = [rotate(k_all, p) * (RET_HEAD_DIM ** -0.5) for p in range(HEAD_PAIRS)]
    st = [state[p] for p in range(HEAD_PAIRS)]

    def scores_of(c, p):
        rows = slice(c * CHUNK, (c + 1) * CHUNK)
        kc = k_rot[p][rows]
        k2 = jnp.concatenate([jnp.where(qk_head_a, kc, 0.0), jnp.where(qk_head_a, 0.0, kc)],
                             axis=0).astype(jnp.bfloat16)
        return lax.dot_general(q_rot[p][rows].astype(jnp.bfloat16), k2, (((1,), (1,)), ((), ())),
                               preferred_element_type=jnp.float32) * dec_ref[p]

    slice_w = 2 * LANES
    n_slices = D_MODEL // slice_w
    fillers = ([functools.partial(proj, OFF_SB + j * slice_w, slice_w) for j in range(n_slices)]
               + [functools.partial(lambda j: _dot(ya, wa_ref[:, j * slice_w:(j + 1) * slice_w]), j)
                  for j in range(n_slices)])
    filled = []

    scores = scores_of(*tasks[0])
    for i, (c, p) in enumerate(tasks):
        rows = slice(c * CHUNK, (c + 1) * CHUNK)
        lanes_p = slice(p * LANES, (p + 1) * LANES)
        qc, kc, vc = q_rot[p][rows], k_rot[p][rows], v_all[rows, lanes_p]
        v2 = jnp.concatenate([jnp.where(v_head_a, vc, 0.0), jnp.where(v_head_a, 0.0, vc)],
                             axis=0).astype(jnp.bfloat16)
        lhs = jnp.concatenate([scores, qc * qdec_ref[p]], axis=1).astype(jnp.bfloat16)
        rhs = jnp.concatenate([v2, st[p].astype(jnp.bfloat16)], axis=0)
        if i + 1 < len(tasks):
            scores = scores_of(*tasks[i + 1])
        if i % 2 == 0 and fillers:
            filled.append(fillers.pop(0)())
        ret = jnp.dot(lhs, rhs, preferred_element_type=jnp.float32)
        kv = lax.dot_general((kc * kdec_ref[p]).astype(jnp.bfloat16), vc.astype(jnp.bfloat16),
                             (((0,), (0,)), ((), ())), preferred_element_type=jnp.float32)
        st[p] = cd_ref[p] * st[p] + smask_ref[...] * kv
        sq = ret * ret
        sum_a = jnp.sum(jnp.where(v_head_a, sq, 0.0), axis=-1, keepdims=True)
        sum_b = jnp.sum(jnp.where(v_head_a, 0.0, sq), axis=-1, keepdims=True)
        ms = jnp.where(v_head_a, sum_a, sum_b) * (1.0 / RET_HEAD_DIM)
        gr = gr_all[rows, lanes_p]
        yb_s[rows, lanes_p] = (gr * _sigmoid(gr)) * (ret * lax.rsqrt(ms + EPS))
    while fillers:
        filled.append(fillers.pop(0)())
    for p in range(HEAD_PAIRS):
        state[p] = st[p]
    sb = jnp.concatenate(filled[:n_slices], axis=1)
    za = jnp.concatenate(filled[n_slices:], axis=1)

    zb = _dot(yb_s[...].astype(jnp.bfloat16), wb_ref[...])
    m = _sigmoid(sa) * za + _sigmoid(sb) * zb
    o_ref[...] = x + _dot(m.astype(jnp.bfloat16), wout_ref[...])


def _ffn_kernel(x_ref, g2_ref, w1_ref, w2_ref, gf_ref, o_ref, *, final_norm):
    x = x_ref[...]
    hb = _rms(x, g2_ref[...]).astype(jnp.bfloat16)
    acc = x
    for j in range(D_FF // FF_BLOCK):
        cols = slice(j * FF_BLOCK, (j + 1) * FF_BLOCK)
        t = jnp.maximum(_dot(hb, w1_ref[:, cols]), 0.0)
        acc = acc + _dot((t * t).astype(jnp.bfloat16), w2_ref[cols, :])
    o_ref[...] = _rms(acc, gf_ref[...]) if final_norm else acc


def _pair_layout(w):
    rows = w.shape[0]
    w = w.reshape(rows, HEAD_PAIRS, 2, 2, HALF)
    return jnp.swapaxes(w, 2, 3).reshape(rows, RET_WIDTH)


def _retention_tables(seq):
    log_g = np.log1p(-np.exp2(-5.0 - np.arange(RET_HEADS, dtype=np.float64)))
    pos = np.arange(CHUNK, dtype=np.float64)
    rel = pos[:, None] - pos[None, :]
    intra = np.where(rel[None] >= 0, np.exp(log_g[:, None, None] * np.maximum(rel, 0.0)[None]), 0.0)
    q_decay = np.exp(log_g[:, None] * (pos[None, :] + 1.0))
    k_decay = np.exp(log_g[:, None] * (CHUNK - 1.0 - pos[None, :]))
    chunk_decay = np.exp(log_g * CHUNK)

    lane = np.arange(LANES)
    qk_sel = (lane // HALF) % 2
    v_sel = lane // RET_HEAD_DIM
    pairs = np.arange(HEAD_PAIRS)
    dec = np.concatenate([intra[0::2], intra[1::2]], axis=2)
    head_of_lane = 2 * pairs[:, None] + qk_sel[None, :]
    qdec = np.transpose(q_decay[head_of_lane], (0, 2, 1))
    kdec = np.transpose(k_decay[head_of_lane], (0, 2, 1))
    smask = (qk_sel[:, None] == v_sel[None, :]).astype(np.float64)
    cd = chunk_decay[head_of_lane][:, :, None] * smask[None]

    inv = ROPE_BASE ** (-np.arange(HALF, dtype=np.float64) / HALF)
    ang = np.arange(seq, dtype=np.float64)[:, None] * inv[None, :]
    cos, sin = np.cos(ang), np.sin(ang)
    cos_t = np.tile(cos, (1, LANES // HALF))
    sin_t = np.concatenate([-sin, -sin, sin, sin], axis=1)
    return tuple(jnp.asarray(t, jnp.float32) for t in (cos_t, sin_t, dec, qdec, kdec, cd, smask))


def _gate_weights(w):
    per_half = LRU_BLOCKS // 2
    rows = w.reshape(2, per_half * LRU_BLOCK_DIM, LRU_BLOCK_DIM)
    tiled = jnp.tile(rows, (1, 1, per_half))
    blk = np.arange(per_half * LRU_BLOCK_DIM) // LRU_BLOCK_DIM
    return tiled * jnp.asarray(blk[:, None] == blk[None, :], w.dtype)


def _full(shape):
    return pl.BlockSpec(shape, lambda *_: (0,) * len(shape))


def kernel(x, norm1_g, w_in, conv_w, conv_b, lru_wr, lru_br, lru_wi, lru_bi, lru_lambda,
           w_branch_a, w_branch_b, w_out, norm2_g, w_ff1, w_ff2, norm_f_g):
    bsz, seq, _ = x.shape
    depth = w_in.shape[0]
    bf = jnp.bfloat16
    cos_t, sin_t, dec, qdec, kdec, cd, smask = _retention_tables(seq)
    ts = SEQ_TILE
    half = LRU_WIDTH // 2

    def row_param(p):
        return p.reshape(depth, 1, -1)

    def layer_row(l, width):
        return pl.BlockSpec((None, 1, width), lambda *_: (l, 0, 0))

    for l in range(depth):
        wl = w_in[l].astype(bf)
        win = jnp.concatenate(
            [wl[:, :OFF_Q], _pair_layout(wl[:, OFF_Q:OFF_K]), _pair_layout(wl[:, OFF_K:OFF_V]),
             wl[:, OFF_V:]], axis=1)
        wg = jnp.concatenate([_gate_weights(lru_wr[l]), _gate_weights(lru_wi[l])], axis=2).astype(bf)

        x = pl.pallas_call(
            _mixer_kernel,
            name="mixer",
            grid=(bsz, seq // ts),
            in_specs=[
                pl.BlockSpec((None, ts, D_MODEL), lambda b, s: (b, s, 0)),
                layer_row(l, D_MODEL),
                _full((D_MODEL, D_IN)),
                pl.BlockSpec((None, CONV_WIDTH, LRU_WIDTH), lambda *_: (l, 0, 0)),
                layer_row(l, LRU_WIDTH),
                _full((2, half, LRU_WIDTH)),
                layer_row(l, LRU_WIDTH),
                layer_row(l, LRU_WIDTH),
                layer_row(l, LRU_WIDTH),
                _full((LRU_WIDTH, D_MODEL)),
                _full((RET_WIDTH, D_MODEL)),
                _full((D_MODEL, D_MODEL)),
                pl.BlockSpec((ts, LANES), lambda b, s: (s, 0)),
                pl.BlockSpec((ts, LANES), lambda b, s: (s, 0)),
                _full((HEAD_PAIRS, CHUNK, 2 * CHUNK)),
                _full((HEAD_PAIRS, CHUNK, LANES)),
                _full((HEAD_PAIRS, CHUNK, LANES)),
                _full((HEAD_PAIRS, LANES, LANES)),
                _full((LANES, LANES)),
            ],
            out_specs=pl.BlockSpec((None, ts, D_MODEL), lambda b, s: (b, s, 0)),
            out_shape=jax.ShapeDtypeStruct((bsz, seq, D_MODEL), jnp.float32),
            scratch_shapes=[
                pltpu.VMEM((SUBLANES + ts, LRU_WIDTH), jnp.float32),
                pltpu.VMEM((1, LRU_WIDTH), jnp.float32),
                pltpu.VMEM((HEAD_PAIRS, LANES, LANES), jnp.float32),
                pltpu.VMEM((ts, RET_WIDTH), jnp.float32),
            ],
            compiler_params=pltpu.CompilerParams(
                dimension_semantics=("arbitrary", "arbitrary"),
                vmem_limit_bytes=VMEM_LIMIT_BYTES),
        )(x, row_param(norm1_g), win, conv_w, row_param(conv_b), wg,
          row_param(lru_br), row_param(lru_bi), row_param(lru_lambda),
          w_branch_a[l].astype(bf), w_branch_b[l].astype(bf), w_out[l].astype(bf),
          cos_t, sin_t, dec, qdec, kdec, cd, smask)

        tokens = bsz * seq
        xf = pl.pallas_call(
            functools.partial(_ffn_kernel, final_norm=(l == depth - 1)),
            name="ffn",
            grid=(tokens // FFN_TILE,),
            in_specs=[
                pl.BlockSpec((FFN_TILE, D_MODEL), lambda i: (i, 0)),
                layer_row(l, D_MODEL),
                _full((D_MODEL, D_FF)),
                _full((D_FF, D_MODEL)),
                _full((1, D_MODEL)),
            ],
            out_specs=pl.BlockSpec((FFN_TILE, D_MODEL), lambda i: (i, 0)),
            out_shape=jax.ShapeDtypeStruct((tokens, D_MODEL), jnp.float32),
            compiler_params=pltpu.CompilerParams(
                dimension_semantics=("arbitrary",),
                vmem_limit_bytes=VMEM_LIMIT_BYTES),
        )(x.reshape(tokens, D_MODEL), row_param(norm2_g), w_ff1[l].astype(bf), w_ff2[l].astype(bf),
          norm_f_g[None])
        x = xf.reshape(bsz, seq, D_MODEL)
    return x
```

```python
import functools

import jax
import jax.numpy as jnp
import numpy as np
from jax import lax
from jax.experimental import pallas as pl
from jax.experimental.pallas import tpu as pltpu

D_MODEL = 1024
LRU_WIDTH = 512
LRU_BLOCKS = 8
LRU_BLOCK_DIM = LRU_WIDTH // LRU_BLOCKS
CONV_WIDTH = 4
RG_C = 8.0
RET_HEADS = 8
RET_HEAD_DIM = 64
RET_WIDTH = RET_HEADS * RET_HEAD_DIM
CHUNK = 128
ROPE_BASE = 10000.0
D_FF = 4 * D_MODEL
EPS = 1e-6

SUBLANES = 8
LANES = 128
VMEM_LIMIT_BYTES = 56 * 1024 * 1024

HEAD_PAIRS = RET_WIDTH // LANES
HALF = RET_HEAD_DIM // 2

OFF_XA = 0
OFF_GA = OFF_XA + LRU_WIDTH
OFF_Q = OFF_GA + LRU_WIDTH
OFF_K = OFF_Q + RET_WIDTH
OFF_V = OFF_K + RET_WIDTH
OFF_GR = OFF_V + RET_WIDTH
OFF_SA = OFF_GR + RET_WIDTH
OFF_SB = OFF_SA + D_MODEL
D_IN = OFF_SB + D_MODEL

SEQ_TILE = 512
FFN_TILE = 1024
FFN_ROWS = 512
FF_BLOCK = 1024


def _dot(a, b):
    return jnp.dot(a, b, preferred_element_type=jnp.float32)


def _rms(x, g):
    return x * lax.rsqrt(jnp.mean(x * x, axis=-1, keepdims=True) + EPS) * g


def _sigmoid(x):
    return 1.0 / (1.0 + jnp.exp(-x))


def _gelu_tanh(x):
    c = np.float32(np.sqrt(2.0 / np.pi))
    return 0.5 * x * (1.0 + jnp.tanh(c * (x + 0.044715 * (x * x * x))))


def _mixer_kernel(x_ref, g1_ref, win_ref, convw_ref, convb_ref, wg_ref, br_ref, bi_ref,
                  lam_ref, wa_ref, wb_ref, wout_ref, cos_ref, sin_ref, dec_ref, qdec_ref,
                  kdec_ref, cd_ref, smask_ref, o_ref,
                  xa_ext, hcar, state, yb_s):
    ts = x_ref.shape[0]
    groups = ts // SUBLANES

    @pl.when(pl.program_id(1) == 0)
    def _():
        xa_ext[0:SUBLANES, :] = jnp.zeros((SUBLANES, LRU_WIDTH), jnp.float32)
        hcar[...] = jnp.zeros_like(hcar)
        state[...] = jnp.zeros_like(state)

    x = x_ref[...]
    hb = _rms(x, g1_ref[...]).astype(jnp.bfloat16)

    def proj(off, n):
        return _dot(hb, win_ref[:, off:off + n])

    xa = proj(OFF_XA, LRU_WIDTH)
    k_all = proj(OFF_K, RET_WIDTH)
    v_all = proj(OFF_V, RET_WIDTH)
    q_all = proj(OFF_Q, RET_WIDTH)

    xa_ext[SUBLANES:SUBLANES + ts, :] = xa
    xc = convb_ref[...] + convw_ref[CONV_WIDTH - 1:CONV_WIDTH, :] * xa
    for j in range(CONV_WIDTH - 1):
        lo = SUBLANES - (CONV_WIDTH - 1) + j
        xc = xc + convw_ref[j:j + 1, :] * xa_ext[lo:lo + ts, :]
    xa_ext[0:SUBLANES, :] = xa[ts - SUBLANES:ts, :]

    xcb = xc.astype(jnp.bfloat16)
    half = LRU_WIDTH // 2
    g0 = _dot(xcb[:, :half], wg_ref[0])
    g1 = _dot(xcb[:, half:], wg_ref[1])
    gr_all = proj(OFF_GR, RET_WIDTH)
    ga = proj(OFF_GA, LRU_WIDTH)
    sa = proj(OFF_SA, D_MODEL)

    r = _sigmoid(jnp.concatenate([g0[:, :half], g1[:, :half]], axis=1) + br_ref[...])
    gate_i = _sigmoid(jnp.concatenate([g0[:, half:], g1[:, half:]], axis=1) + bi_ref[...])

    z = -lam_ref[...]
    softplus = jnp.maximum(z, 0.0) + jnp.log1p(jnp.exp(-jnp.abs(z)))
    log_a = (-RG_C) * r * softplus
    a = jnp.exp(log_a)
    mult = jnp.sqrt(-jnp.tanh(log_a) * (a * a + 1.0))
    u = mult * (gate_i * xc)

    a3 = a.reshape(groups, SUBLANES, LRU_WIDTH)
    u3 = u.reshape(groups, SUBLANES, LRU_WIDTH)
    row = lax.broadcasted_iota(jnp.int32, (groups, SUBLANES, LRU_WIDTH), 1)
    shift = 1
    while shift < SUBLANES:
        keep = row >= shift
        a_prev = jnp.where(keep, pltpu.roll(a3, shift, 1), 1.0)
        u_prev = jnp.where(keep, pltpu.roll(u3, shift, 1), 0.0)
        u3 = a3 * u_prev + u3
        a3 = a3 * a_prev
        shift *= 2
    a2 = a3.reshape(ts, LRU_WIDTH)
    u2 = u3.reshape(ts, LRU_WIDTH)
    carry = jnp.broadcast_to(hcar[...], (SUBLANES, LRU_WIDTH))
    h_groups = []
    for g in range(groups):
        rows = slice(g * SUBLANES, (g + 1) * SUBLANES)
        hg = u2[rows] + a2[rows] * carry
        h_groups.append(hg)
        carry = jnp.broadcast_to(hg[SUBLANES - 1:SUBLANES, :], (SUBLANES, LRU_WIDTH))
    hcar[...] = carry[0:1, :]
    h = jnp.concatenate(h_groups, axis=0)

    ya = (_gelu_tanh(ga) * h).astype(jnp.bfloat16)

    lane = lax.broadcasted_iota(jnp.int32, (1, LANES), 1)
    qk_head_a = (lane // HALF) % 2 == 0
    v_head_a = lane < RET_HEAD_DIM
    cosv = cos_ref[...]
    sinv = sin_ref[...]
    tasks = [(c, p) for c in range(ts // CHUNK) for p in range(HEAD_PAIRS)]

    def rotate(t_all, p):
        tp = t_all[:, p * LANES:(p + 1) * LANES]
        return tp * cosv + pltpu.roll(tp, LANES // 2, 1) * sinv

    q_rot = [rotate(q_all, p) for p in range(HEAD_PAIRS)]
    k_rot = [rotate(k_all, p) * (RET_HEAD_DIM ** -0.5) for p in range(HEAD_PAIRS)]
    st = [state[p] for p in range(HEAD_PAIRS)]

    def scores_of(c, p):
        rows = slice(c * CHUNK, (c + 1) * CHUNK)
        kc = k_rot[p][rows]
        k2 = jnp.concatenate([jnp.where(qk_head_a, kc, 0.0), jnp.where(qk_head_a, 0.0, kc)],
                             axis=0).astype(jnp.bfloat16)
        return lax.dot_general(q_rot[p][rows].astype(jnp.bfloat16), k2, (((1,), (1,)), ((), ())),
                               preferred_element_type=jnp.float32) * dec_ref[p]

    slice_w = 2 * LANES
    n_slices = D_MODEL // slice_w
    fillers = ([functools.partial(proj, OFF_SB + j * slice_w, slice_w) for j in range(n_slices)]
               + [functools.partial(lambda j: _dot(ya, wa_ref[:, j * slice_w:(j + 1) * slice_w]), j)
                  for j in range(n_slices)])
    filled = []

    scores = scores_of(*tasks[0])
    for i, (c, p) in enumerate(tasks):
        rows = slice(c * CHUNK, (c + 1) * CHUNK)
        lanes_p = slice(p * LANES, (p + 1) * LANES)
        qc, kc, vc = q_rot[p][rows], k_rot[p][rows], v_all[rows, lanes_p]
        v2 = jnp.concatenate([jnp.where(v_head_a, vc, 0.0), jnp.where(v_head_a, 0.0, vc)],
                             axis=0).astype(jnp.bfloat16)
        lhs = jnp.concatenate([scores, qc * qdec_ref[p]], axis=1).astype(jnp.bfloat16)
        rhs = jnp.concatenate([v2, st[p].astype(jnp.bfloat16)], axis=0)
        if i + 1 < len(tasks):
            scores = scores_of(*tasks[i + 1])
        if i % 2 == 0 and fillers:
            filled.append(fillers.pop(0)())
        ret = jnp.dot(lhs, rhs, preferred_element_type=jnp.float32)
        kv = lax.dot_general((kc * kdec_ref[p]).astype(jnp.bfloat16), vc.astype(jnp.bfloat16),
                             (((0,), (0,)), ((), ())), preferred_element_type=jnp.float32)
        st[p] = cd_ref[p] * st[p] + smask_ref[...] * kv
        sq = ret * ret
        sum_a = jnp.sum(jnp.where(v_head_a, sq, 0.0), axis=-1, keepdims=True)
        sum_b = jnp.sum(jnp.where(v_head_a, 0.0, sq), axis=-1, keepdims=True)
        ms = jnp.where(v_head_a, sum_a, sum_b) * (1.0 / RET_HEAD_DIM)
        gr = gr_all[rows, lanes_p]
        yb_s[rows, lanes_p] = (gr * _sigmoid(gr)) * (ret * lax.rsqrt(ms + EPS))
    while fillers:
        filled.append(fillers.pop(0)())
    for p in range(HEAD_PAIRS):
        state[p] = st[p]
    sb = jnp.concatenate(filled[:n_slices], axis=1)
    za = jnp.concatenate(filled[n_slices:], axis=1)

    zb = _dot(yb_s[...].astype(jnp.bfloat16), wb_ref[...])
    m = _sigmoid(sa) * za + _sigmoid(sb) * zb
    o_ref[...] = x + _dot(m.astype(jnp.bfloat16), wout_ref[...])


def _ffn_kernel(x_ref, g2_ref, w1_ref, w2_ref, gf_ref, o_ref, *, final_norm):
    for i in range(x_ref.shape[0] // FFN_ROWS):
        rows = slice(i * FFN_ROWS, (i + 1) * FFN_ROWS)
        x = x_ref[rows, :]
        hb = _rms(x, g2_ref[...]).astype(jnp.bfloat16)
        acc = x
        for j in range(D_FF // FF_BLOCK):
            cols = slice(j * FF_BLOCK, (j + 1) * FF_BLOCK)
            t = jnp.maximum(_dot(hb, w1_ref[:, cols]), 0.0)
            acc = acc + _dot((t * t).astype(jnp.bfloat16), w2_ref[cols, :])
        o_ref[rows, :] = _rms(acc, gf_ref[...]) if final_norm else acc


def _pair_layout(w):
    rows = w.shape[0]
    w = w.reshape(rows, HEAD_PAIRS, 2, 2, HALF)
    return jnp.swapaxes(w, 2, 3).reshape(rows, RET_WIDTH)


def _retention_tables(seq):
    log_g = np.log1p(-np.exp2(-5.0 - np.arange(RET_HEADS, dtype=np.float64)))
    pos = np.arange(CHUNK, dtype=np.float64)
    rel = pos[:, None] - pos[None, :]
    intra = np.where(rel[None] >= 0, np.exp(log_g[:, None, None] * np.maximum(rel, 0.0)[None]), 0.0)
    q_decay = np.exp(log_g[:, None] * (pos[None, :] + 1.0))
    k_decay = np.exp(log_g[:, None] * (CHUNK - 1.0 - pos[None, :]))
    chunk_decay = np.exp(log_g * CHUNK)

    lane = np.arange(LANES)
    qk_sel = (lane // HALF) % 2
    v_sel = lane // RET_HEAD_DIM
    pairs = np.arange(HEAD_PAIRS)
    dec = np.concatenate([intra[0::2], intra[1::2]], axis=2)
    head_of_lane = 2 * pairs[:, None] + qk_sel[None, :]
    qdec = np.transpose(q_decay[head_of_lane], (0, 2, 1))
    kdec = np.transpose(k_decay[head_of_lane], (0, 2, 1))
    smask = (qk_sel[:, None] == v_sel[None, :]).astype(np.float64)
    cd = chunk_decay[head_of_lane][:, :, None] * smask[None]

    inv = ROPE_BASE ** (-np.arange(HALF, dtype=np.float64) / HALF)
    ang = np.arange(seq, dtype=np.float64)[:, None] * inv[None, :]
    cos, sin = np.cos(ang), np.sin(ang)
    cos_t = np.tile(cos, (1, LANES // HALF))
    sin_t = np.concatenate([-sin, -sin, sin, sin], axis=1)
    return tuple(jnp.asarray(t, jnp.float32) for t in (cos_t, sin_t, dec, qdec, kdec, cd, smask))


def _gate_weights(w):
    per_half = LRU_BLOCKS // 2
    rows = w.reshape(2, per_half * LRU_BLOCK_DIM, LRU_BLOCK_DIM)
    tiled = jnp.tile(rows, (1, 1, per_half))
    blk = np.arange(per_half * LRU_BLOCK_DIM) // LRU_BLOCK_DIM
    return tiled * jnp.asarray(blk[:, None] == blk[None, :], w.dtype)


def _full(shape):
    return pl.BlockSpec(shape, lambda *_: (0,) * len(shape), pipeline_mode=pl.Buffered(1))


def kernel(x, norm1_g, w_in, conv_w, conv_b, lru_wr, lru_br, lru_wi, lru_bi, lru_lambda,
           w_branch_a, w_branch_b, w_out, norm2_g, w_ff1, w_ff2, norm_f_g):
    bsz, seq, _ = x.shape
    depth = w_in.shape[0]
    bf = jnp.bfloat16
    cos_t, sin_t, dec, qdec, kdec, cd, smask = _retention_tables(seq)
    ts = SEQ_TILE
    half = LRU_WIDTH // 2

    def row_param(p):
        return p.reshape(depth, 1, -1)

    def layer_row(l, width):
        return pl.BlockSpec((None, 1, width), lambda *_: (l, 0, 0))

    for l in range(depth):
        wl = w_in[l].astype(bf)
        win = jnp.concatenate(
            [wl[:, :OFF_Q], _pair_layout(wl[:, OFF_Q:OFF_K]), _pair_layout(wl[:, OFF_K:OFF_V]),
             wl[:, OFF_V:]], axis=1)
        wg = jnp.concatenate([_gate_weights(lru_wr[l]), _gate_weights(lru_wi[l])], axis=2).astype(bf)

        x = pl.pallas_call(
            _mixer_kernel,
            name="mixer",
            grid=(bsz, seq // ts),
            in_specs=[
                pl.BlockSpec((None, ts, D_MODEL), lambda b, s: (b, s, 0)),
                layer_row(l, D_MODEL),
                _full((D_MODEL, D_IN)),
                pl.BlockSpec((None, CONV_WIDTH, LRU_WIDTH), lambda *_: (l, 0, 0)),
                layer_row(l, LRU_WIDTH),
                _full((2, half, LRU_WIDTH)),
                layer_row(l, LRU_WIDTH),
                layer_row(l, LRU_WIDTH),
                layer_row(l, LRU_WIDTH),
                _full((LRU_WIDTH, D_MODEL)),
                _full((RET_WIDTH, D_MODEL)),
                _full((D_MODEL, D_MODEL)),
                pl.BlockSpec((ts, LANES), lambda b, s: (s, 0)),
                pl.BlockSpec((ts, LANES), lambda b, s: (s, 0)),
                _full((HEAD_PAIRS, CHUNK, 2 * CHUNK)),
                _full((HEAD_PAIRS, CHUNK, LANES)),
                _full((HEAD_PAIRS, CHUNK, LANES)),
                _full((HEAD_PAIRS, LANES, LANES)),
                _full((LANES, LANES)),
            ],
            out_specs=pl.BlockSpec((None, ts, D_MODEL), lambda b, s: (b, s, 0)),
            out_shape=jax.ShapeDtypeStruct((bsz, seq, D_MODEL), jnp.float32),
            scratch_shapes=[
                pltpu.VMEM((SUBLANES + ts, LRU_WIDTH), jnp.float32),
                pltpu.VMEM((1, LRU_WIDTH), jnp.float32),
                pltpu.VMEM((HEAD_PAIRS, LANES, LANES), jnp.float32),
                pltpu.VMEM((ts, RET_WIDTH), jnp.float32),
            ],
            compiler_params=pltpu.CompilerParams(
                dimension_semantics=("arbitrary", "arbitrary"),
                vmem_limit_bytes=VMEM_LIMIT_BYTES),
        )(x, row_param(norm1_g), win, conv_w, row_param(conv_b), wg,
          row_param(lru_br), row_param(lru_bi), row_param(lru_lambda),
          w_branch_a[l].astype(bf), w_branch_b[l].astype(bf), w_out[l].astype(bf),
          cos_t, sin_t, dec, qdec, kdec, cd, smask)

        tokens = bsz * seq
        xf = pl.pallas_call(
            functools.partial(_ffn_kernel, final_norm=(l == depth - 1)),
            name="ffn",
            grid=(tokens // FFN_TILE,),
            in_specs=[
                pl.BlockSpec((FFN_TILE, D_MODEL), lambda i: (i, 0)),
                layer_row(l, D_MODEL),
                _full((D_MODEL, D_FF)),
                _full((D_FF, D_MODEL)),
                _full((1, D_MODEL)),
            ],
            out_specs=pl.BlockSpec((FFN_TILE, D_MODEL), lambda i: (i, 0)),
            out_shape=jax.ShapeDtypeStruct((tokens, D_MODEL), jnp.float32),
            compiler_params=pltpu.CompilerParams(
                dimension_semantics=("arbitrary",),
                vmem_limit_bytes=VMEM_LIMIT_BYTES),
        )(x.reshape(tokens, D_MODEL), row_param(norm2_g), w_ff1[l].astype(bf), w_ff2[l].astype(bf),
          norm_f_g[None])
        x = xf.reshape(bsz, seq, D_MODEL)
    return x
```

```python
import functools

import jax
import jax.numpy as jnp
import numpy as np
from jax import lax
from jax.experimental import pallas as pl
from jax.experimental.pallas import tpu as pltpu

D_MODEL = 1024
LRU_WIDTH = 512
LRU_BLOCKS = 8
LRU_BLOCK_DIM = LRU_WIDTH // LRU_BLOCKS
CONV_WIDTH = 4
RG_C = 8.0
RET_HEADS = 8
RET_HEAD_DIM = 64
RET_WIDTH = RET_HEADS * RET_HEAD_DIM
CHUNK = 128
ROPE_BASE = 10000.0
D_FF = 4 * D_MODEL
EPS = 1e-6

SUBLANES = 8
LANES = 128
BF16_ROWS = 16
VMEM_LIMIT_BYTES = 56 * 1024 * 1024

HEAD_PAIRS = RET_WIDTH // LANES
HALF = RET_HEAD_DIM // 2

OFF_XA = 0
OFF_GA = OFF_XA + LRU_WIDTH
OFF_Q = OFF_GA + LRU_WIDTH
OFF_K = OFF_Q + RET_WIDTH
OFF_V = OFF_K + RET_WIDTH
OFF_GR = OFF_V + RET_WIDTH
OFF_SA = OFF_GR + RET_WIDTH
OFF_SB = OFF_SA + D_MODEL
D_IN = OFF_SB + D_MODEL

SEQ_TILE = 512
FFN_TILE = 1024
STAGE_BYTES = 2 * 1024 * 1024
IN_STAGE_ROWS = 64
FFN_ROWS = 512
FF_BLOCK = 1024
SCORE_LOOKAHEAD = 1


def _dot(a, b):
    return jnp.dot(a, b, preferred_element_type=jnp.float32)


def _pack_rows(w):
    k, n = w.shape
    w = w.astype(jnp.bfloat16).reshape(k // BF16_ROWS, BF16_ROWS, n // LANES, LANES)
    return jnp.swapaxes(w, 1, 2)


def _unpack_rows(tiles):
    return jnp.concatenate(
        [jnp.concatenate([tiles[i, j] for j in range(tiles.shape[1])], axis=1)
         for i in range(tiles.shape[0])], axis=0)


def _rms(x, g):
    return x * lax.rsqrt(jnp.mean(x * x, axis=-1, keepdims=True) + EPS) * g


NEG_LOG2_E = np.float32(-np.log2(np.e))


def _exp_neg(x):
    return jnp.exp2(x * NEG_LOG2_E)


def _sigmoid(x):
    return 1.0 / (1.0 + _exp_neg(x))


def _gelu_tanh_times(x, y):
    c = np.float32(np.sqrt(2.0 / np.pi))
    return (0.5 * x * (1.0 + jnp.tanh(c * (x + 0.044715 * (x * x * x))))) * y


def _mixer_kernel(x_ref, g1_ref, win_hbm, convw_ref, convb_ref, wg_ref, br_ref, bi_ref,
                  lam_ref, wa_hbm, wb_hbm, wout_hbm, cos_ref, sin_ref, dec_ref, qdec_ref,
                  kdec_ref, cd_ref, smask_ref, o_ref,
                  xa_ext, hcar, state, yb_s, win_ref, wa_ref, wb_ref, wout_ref,
                  stage_in, stage_sq, sem, *, layer):
    ts = x_ref.shape[0]

    @pl.when((pl.program_id(0) == 0) & (pl.program_id(1) == 0))
    def _():
        qk_tiles = tuple(range(OFF_Q // LANES, OFF_V // LANES))
        _ingest_weight(win_hbm.at[layer], win_ref, stage_in, sem, pair_layout_cols=qk_tiles)
        _ingest_weight(wa_hbm.at[layer], wa_ref, stage_sq, sem)
        _ingest_weight(wb_hbm.at[layer], wb_ref, stage_sq, sem)
        _ingest_weight(wout_hbm.at[layer], wout_ref, stage_sq, sem)

    @pl.when(pl.program_id(1) == 0)
    def _():
        xa_ext[0:SUBLANES, :] = jnp.zeros((SUBLANES, LRU_WIDTH), jnp.float32)
        hcar[...] = jnp.zeros_like(hcar)
        state[...] = jnp.zeros_like(state)

    x = x_ref[...]
    hb = _rms(x, g1_ref[...]).astype(jnp.bfloat16)

    def proj(off, n):
        return _dot(hb, _unpack_rows(win_ref[:, off // LANES:(off + n) // LANES]))

    xa = proj(OFF_XA, LRU_WIDTH)
    k_all = proj(OFF_K, RET_WIDTH)

    xa_ext[SUBLANES:SUBLANES + ts, :] = xa
    xc = convb_ref[...] + convw_ref[CONV_WIDTH - 1:CONV_WIDTH, :] * xa
    for j in range(CONV_WIDTH - 1):
        lo = SUBLANES - (CONV_WIDTH - 1) + j
        xc = xc + convw_ref[j:j + 1, :] * xa_ext[lo:lo + ts, :]
    xa_ext[0:SUBLANES, :] = xa[ts - SUBLANES:ts, :]
    xcb = xc.astype(jnp.bfloat16)
    half = LRU_WIDTH // 2
    groups = CHUNK // SUBLANES
    z = -lam_ref[...]
    decay_rate = RG_C * (jnp.maximum(z, 0.0) + jnp.log1p(jnp.exp(-jnp.abs(z))))
    row =lax.broadcasted_iota(jnp.int32, (groups, SUBLANES, LRU_WIDTH), 1)

    def lru_chunk(c, carry):
        rows = slice(c * CHUNK, (c + 1) * CHUNK)
        g0 = _dot(xcb[rows, :half], _unpack_rows(wg_ref[0]))
        g1 = _dot(xcb[rows, half:], _unpack_rows(wg_ref[1]))
        r = _sigmoid(jnp.concatenate([g0[:, :half], g1[:, :half]], axis=1) + br_ref[...])
        gate_i = _sigmoid(jnp.concatenate([g0[:, half:], g1[:, half:]], axis=1) + bi_ref[...])
        neg_log_a = r * decay_rate
        a = _exp_neg(neg_log_a)
        one_minus_a2 = jnp.tanh(neg_log_a) * (a * a + 1.0)
        mult = jnp.where(one_minus_a2 > 0.0, one_minus_a2 * lax.rsqrt(one_minus_a2), 0.0)
        u = mult * (gate_i * xc[rows])

        a3 = a.reshape(groups, SUBLANES, LRU_WIDTH)
        u3 = u.reshape(groups, SUBLANES, LRU_WIDTH)
        shift = 1
        while shift < SUBLANES:
            keep = row >= shift
            a_prev = jnp.where(keep, pltpu.roll(a3, shift, 1), 1.0)
            u_prev = jnp.where(keep, pltpu.roll(u3, shift, 1), 0.0)
            u3 = a3 * u_prev + u3
            a3 = a3 * a_prev
            shift *= 2
        h_groups = []
        for g in range(groups):
            hg = u3[g] + a3[g] * carry
            h_groups.append(hg)
            carry = jnp.broadcast_to(hg[SUBLANES - 1:SUBLANES, :], (SUBLANES, LRU_WIDTH))
        return jnp.concatenate(h_groups, axis=0), carry

    slice_w = 2 * LANES
    n_chunks = ts // CHUNK
    wide_offs = [off + j * slice_w for off in (OFF_Q, OFF_V, OFF_GR, OFF_GA)
                 for j in range(RET_WIDTH // slice_w)]
    per_chunk = len(wide_offs) // n_chunks
    carry = jnp.broadcast_to(hcar[...], (SUBLANES, LRU_WIDTH))
    h_chunks = []
    wide = []
    for c in range(n_chunks):
        h_c, carry = lru_chunk(c, carry)
        h_chunks.append(h_c)
        wide += [proj(off, slice_w) for off in wide_offs[c * per_chunk:(c + 1) * per_chunk]]
    hcar[...] = carry[0:1, :]
    n_sl = RET_WIDTH // slice_w
    q_all, v_all, gr_all, ga = (jnp.concatenate(wide[j * n_sl:(j + 1) * n_sl], axis=1)
                                for j in range(4))
    ya = _gelu_tanh_times(ga, jnp.concatenate(h_chunks, axis=0)).astype(jnp.bfloat16)

    lane = lax.broadcasted_iota(jnp.int32, (1, LANES), 1)
    qk_head_a = (lane // HALF) % 2 == 0
    v_head_a = lane < RET_HEAD_DIM
    cosv = cos_ref[...]
    sinv = sin_ref[...]
    tasks = [(c, p) for c in range(ts // CHUNK) for p in range(HEAD_PAIRS)]

    def rotate(t_all, p):
        tp = t_all[:, p * LANES:(p + 1) * LANES]
        return tp * cosv + pltpu.roll(tp, LANES // 2, 1) * sinv

    q_rot = [rotate(q_all, p) for p in range(HEAD_PAIRS)]
    k_rot = [rotate(k_all, p) for p in range(HEAD_PAIRS)]
    st = [state[p] for p in range(HEAD_PAIRS)]

    def scores_of(c, p):
        rows = slice(c * CHUNK, (c + 1) * CHUNK)
        kc = k_rot[p][rows]
        k2 = jnp.concatenate([jnp.where(qk_head_a, kc, 0.0), jnp.where(qk_head_a, 0.0, kc)],
                             axis=0).astype(jnp.bfloat16)
        return lax.dot_general(q_rot[p][rows].astype(jnp.bfloat16), k2, (((1,), (1,)), ((), ())),
                               preferred_element_type=jnp.float32) * dec_ref[p]

    n_wide = D_MODEL // slice_w
    fillers = [functools.partial(proj, off + j * slice_w, slice_w)
               for off in (OFF_SA, OFF_SB) for j in range(n_wide)]
    filled = []
    stride = len(tasks) // len(fillers)

    pending = [scores_of(*t) for t in tasks[:SCORE_LOOKAHEAD]]
    for i, (c, p) in enumerate(tasks):
        scores = pending.pop(0)
        rows = slice(c * CHUNK, (c + 1) * CHUNK)
        lanes_p = slice(p * LANES, (p + 1) * LANES)
        qc, kc, vc = q_rot[p][rows], k_rot[p][rows], v_all[rows, lanes_p]
        vcb = vc.astype(jnp.bfloat16)
        v2 = jnp.concatenate([jnp.where(v_head_a, vc, 0.0), jnp.where(v_head_a, 0.0, vc)],
                             axis=0).astype(jnp.bfloat16)
        lhs = jnp.concatenate([scores, qc * qdec_ref[p]], axis=1).astype(jnp.bfloat16)
        rhs = jnp.concatenate([v2, st[p].astype(jnp.bfloat16)], axis=0)
        if i + SCORE_LOOKAHEAD < len(tasks):
            pending.append(scores_of(*tasks[i + SCORE_LOOKAHEAD]))
        if i % stride == stride - 1:
            filled.append(fillers.pop(0)())
        ret = jnp.dot(lhs, rhs, preferred_element_type=jnp.float32)
        kv = lax.dot_general((kc * kdec_ref[p]).astype(jnp.bfloat16), vcb,
                             (((0,), (0,)), ((), ())), preferred_element_type=jnp.float32)
        st[p] = cd_ref[p] * st[p] + smask_ref[...] * kv
        sq = ret * ret
        sum_a = jnp.sum(jnp.where(v_head_a, sq, 0.0), axis=-1, keepdims=True)
        sum_b = jnp.sum(jnp.where(v_head_a, 0.0, sq), axis=-1, keepdims=True)
        ms = jnp.where(v_head_a, sum_a, sum_b) * (1.0 / RET_HEAD_DIM)
        gr = gr_all[rows, lanes_p]
        yb_s[rows, lanes_p] = (gr * _sigmoid(gr)) * (ret * lax.rsqrt(ms + EPS))
    assert not fillers
    for p in range(HEAD_PAIRS):
        state[p] = st[p]
    sa, sb = (jnp.concatenate(filled[j * n_wide:(j + 1) * n_wide], axis=1) for j in range(2))

    za = _dot(ya, _unpack_rows(wa_ref[...]))
    zb = _dot(yb_s[...].astype(jnp.bfloat16), _unpack_rows(wb_ref[...]))
    m = _sigmoid(sa) * za + _sigmoid(sb) * zb
    o_ref[...] = x + _dot(m.astype(jnp.bfloat16), _unpack_rows(wout_ref[...]))


def _ingest_weight(w_hbm, w_vmem, stage, sem, pair_layout_cols=()):
    rows = stage.shape[1]
    n_chunks = w_hbm.shape[0] // rows

    def chunk_copy(i):
        return pltpu.make_async_copy(w_hbm.at[pl.ds(i * rows, rows)], stage.at[i % 2], sem.at[i % 2])

    chunk_copy(0).start()
    for i in range(n_chunks):
        if i + 1 < n_chunks:
            chunk_copy(i + 1).start()
        chunk_copy(i).wait()
        if len(w_vmem.shape) == 2:
            w_vmem[i * rows:(i + 1) * rows, :] = stage[i % 2].astype(jnp.bfloat16)
            continue
        lane = lax.broadcasted_iota(jnp.int32, (rows, LANES), 1)
        for nb in range(w_vmem.shape[1]):
            slab = stage[i % 2, :, nb * LANES:(nb + 1) * LANES]
            if nb in pair_layout_cols:
                slab = jnp.where(lane // HALF == 1, pltpu.roll(slab, LANES - HALF, 1),
                                 jnp.where(lane // HALF == 2, pltpu.roll(slab, HALF, 1), slab))
            slab = slab.astype(jnp.bfloat16)
            for rb in range(rows // BF16_ROWS):
                w_vmem[i * rows // BF16_ROWS + rb, nb] = slab[rb * BF16_ROWS:(rb + 1) * BF16_ROWS]


def _ffn_kernel(x_ref, g2_ref, w1_hbm, w2_hbm, gf_ref, o_ref, w1_s, w2_s, stage1, stage2, sem,
                *, layer, final_norm):
    @pl.when(pl.program_id(0) == 0)
    def _():
        _ingest_weight(w1_hbm.at[layer], w1_s, stage1, sem)
        _ingest_weight(w2_hbm.at[layer], w2_s, stage2, sem)

    for i in range(x_ref.shape[0] // FFN_ROWS):
        rows = slice(i * FFN_ROWS, (i + 1) * FFN_ROWS)
        x = x_ref[rows, :]
        hb = _rms(x, g2_ref[...]).astype(jnp.bfloat16)
        acc = x
        for j in range(D_FF // FF_BLOCK):
            cols = slice(j * FF_BLOCK, (j + 1) * FF_BLOCK)
            t = jnp.maximum(_dot(hb, w1_s[:, cols]), 0.0)
            acc = acc + _dot((t * t).astype(jnp.bfloat16), w2_s[cols, :])
        o_ref[rows, :] = _rms(acc, gf_ref[...]) if final_norm else acc


def _retention_tables(seq):
    log_g = np.log1p(-np.exp2(-5.0 - np.arange(RET_HEADS, dtype=np.float64)))
    pos = np.arange(CHUNK, dtype=np.float64)
    rel = pos[:, None] - pos[None, :]
    intra = np.where(rel[None] >= 0, np.exp(log_g[:, None, None] * np.maximum(rel, 0.0)[None]), 0.0)
    q_decay = np.exp(log_g[:, None] * (pos[None, :] + 1.0))
    k_decay = np.exp(log_g[:, None] * (CHUNK - 1.0 - pos[None, :]))
    chunk_decay = np.exp(log_g * CHUNK)

    lane = np.arange(LANES)
    qk_sel = (lane // HALF) % 2
    v_sel = lane // RET_HEAD_DIM
    pairs = np.arange(HEAD_PAIRS)
    k_scale = RET_HEAD_DIM ** -0.5
    dec = k_scale * np.concatenate([intra[0::2], intra[1::2]], axis=2)
    head_of_lane = 2 * pairs[:, None] + qk_sel[None, :]
    qdec = k_scale * np.transpose(q_decay[head_of_lane], (0, 2, 1))
    kdec = np.transpose(k_decay[head_of_lane], (0, 2, 1))
    smask = (qk_sel[:, None] == v_sel[None, :]).astype(np.float64)
    cd = chunk_decay[head_of_lane][:, :, None] * smask[None]

    inv = ROPE_BASE ** (-np.arange(HALF, dtype=np.float64) / HALF)
    ang = np.arange(seq, dtype=np.float64)[:, None] * inv[None, :]
    cos, sin = np.cos(ang), np.sin(ang)
    cos_t = np.tile(cos, (1, LANES // HALF))
    sin_t = np.concatenate([-sin, -sin, sin, sin], axis=1)
    return tuple(jnp.asarray(t, jnp.float32) for t in (cos_t, sin_t, dec, qdec, kdec, cd, smask))


def _gate_weights(w):
    per_half = LRU_BLOCKS // 2
    rows = w.reshape(2, per_half * LRU_BLOCK_DIM, LRU_BLOCK_DIM)
    tiled = jnp.tile(rows, (1, 1, per_half))
    blk = np.arange(per_half * LRU_BLOCK_DIM) // LRU_BLOCK_DIM
    return tiled * jnp.asarray(blk[:, None] == blk[None, :], w.dtype)


def _tile_scratch(k, n):
    return pltpu.VMEM((k // BF16_ROWS, n // LANES, BF16_ROWS, LANES), jnp.bfloat16)


def _tiled(k, n, lead=()):
    return _full(lead + (k // BF16_ROWS, n // LANES, BF16_ROWS, LANES))


def _full(shape):
    return pl.BlockSpec(shape, lambda *_: (0,) * len(shape), pipeline_mode=pl.Buffered(1))


def kernel(x, norm1_g, w_in, conv_w, conv_b, lru_wr, lru_br, lru_wi, lru_bi, lru_lambda,
           w_branch_a, w_branch_b, w_out, norm2_g, w_ff1, w_ff2, norm_f_g):
    bsz, seq, _ = x.shape
    depth = w_in.shape[0]
    bf = jnp.bfloat16
    cos_t, sin_t, dec, qdec, kdec, cd, smask = _retention_tables(seq)
    ts = SEQ_TILE
    half = LRU_WIDTH // 2

    def row_param(p):
        return p.reshape(depth, 1, -1)

    def layer_row(l, width):
        return pl.BlockSpec((None, 1, width), lambda *_: (l, 0, 0))

    for l in range(depth):
        wg = jnp.concatenate([_gate_weights(lru_wr[l]), _gate_weights(lru_wi[l])], axis=2)
        wg = jnp.stack([_pack_rows(wg[0]), _pack_rows(wg[1])])

        x = pl.pallas_call(
            functools.partial(_mixer_kernel, layer=l),
            name="mixer",
            grid=(bsz, seq // ts),
            in_specs=[
                pl.BlockSpec((None, ts, D_MODEL), lambda b, s: (b, s, 0)),
                layer_row(l, D_MODEL),
                pl.BlockSpec(memory_space=pl.ANY),
                pl.BlockSpec((None, CONV_WIDTH, LRU_WIDTH), lambda *_: (l, 0, 0)),
                layer_row(l, LRU_WIDTH),
                _tiled(half, LRU_WIDTH, lead=(2,)),
                layer_row(l, LRU_WIDTH),
                layer_row(l, LRU_WIDTH),
                layer_row(l, LRU_WIDTH),
                pl.BlockSpec(memory_space=pl.ANY),
                pl.BlockSpec(memory_space=pl.ANY),
                pl.BlockSpec(memory_space=pl.ANY),
                pl.BlockSpec((ts, LANES), lambda b, s: (s, 0)),
                pl.BlockSpec((ts, LANES), lambda b, s: (s, 0)),
                _full((HEAD_PAIRS, CHUNK, 2 * CHUNK)),
                _full((HEAD_PAIRS, CHUNK, LANES)),
                _full((HEAD_PAIRS, CHUNK, LANES)),
                _full((HEAD_PAIRS, LANES, LANES)),
                _full((LANES, LANES)),
            ],
            out_specs=pl.BlockSpec((None, ts, D_MODEL), lambda b, s: (b, s, 0)),
            out_shape=jax.ShapeDtypeStruct((bsz, seq, D_MODEL), jnp.float32),
            scratch_shapes=[
                pltpu.VMEM((SUBLANES + ts, LRU_WIDTH), jnp.float32),
                pltpu.VMEM((1, LRU_WIDTH), jnp.float32),
                pltpu.VMEM((HEAD_PAIRS, LANES, LANES), jnp.float32),
                pltpu.VMEM((ts, RET_WIDTH), jnp.float32),
                _tile_scratch(D_MODEL, D_IN),
                _tile_scratch(LRU_WIDTH, D_MODEL),
                _tile_scratch(RET_WIDTH, D_MODEL),
                _tile_scratch(D_MODEL, D_MODEL),
                pltpu.VMEM((2, IN_STAGE_ROWS, D_IN), jnp.float32),
                pltpu.VMEM((2, STAGE_BYTES // (4 * D_MODEL), D_MODEL), jnp.float32),
                pltpu.SemaphoreType.DMA((2,)),
            ],
            compiler_params=pltpu.CompilerParams(
                dimension_semantics=("arbitrary", "arbitrary"),
                vmem_limit_bytes=VMEM_LIMIT_BYTES),
        )(x, row_param(norm1_g), w_in, conv_w, row_param(conv_b), wg,
          row_param(lru_br), row_param(lru_bi), row_param(lru_lambda),
          w_branch_a, w_branch_b, w_out,
          cos_t, sin_t, dec, qdec, kdec, cd, smask)

        tokens = bsz * seq
        xf = pl.pallas_call(
            functools.partial(_ffn_kernel, layer=l, final_norm=(l == depth - 1)),
            name="ffn",
            grid=(tokens // FFN_TILE,),
            in_specs=[
                pl.BlockSpec((FFN_TILE, D_MODEL), lambda i: (i, 0)),
                layer_row(l, D_MODEL),
                pl.BlockSpec(memory_space=pl.ANY),
                pl.BlockSpec(memory_space=pl.ANY),
                _full((1, D_MODEL)),
            ],
            out_specs=pl.BlockSpec((FFN_TILE, D_MODEL), lambda i: (i, 0)),
            out_shape=jax.ShapeDtypeStruct((tokens, D_MODEL), jnp.float32),
            scratch_shapes=[
                pltpu.VMEM((D_MODEL, D_FF), bf),
                pltpu.VMEM((D_FF, D_MODEL), bf),
                pltpu.VMEM((2, STAGE_BYTES // (4 * D_FF), D_FF), jnp.float32),
                pltpu.VMEM((2, STAGE_BYTES // (4 * D_MODEL), D_MODEL), jnp.float32),
                pltpu.SemaphoreType.DMA((2,)),
            ],
            compiler_params=pltpu.CompilerParams(
                dimension_semantics=("arbitrary",),
                vmem_limit_bytes=VMEM_LIMIT_BYTES),
        )(x.reshape(tokens, D_MODEL), row_param(norm2_g), w_ff1, w_ff2, norm_f_g[None])
        x = xf.reshape(bsz, seq, D_MODEL)
    return x
```

```python
import functools

import jax
import jax.numpy as jnp
import numpy as np
from jax import lax
from jax.experimental import pallas as pl
from jax.experimental.pallas import tpu as pltpu

D_MODEL = 1024
LRU_WIDTH = 512
LRU_BLOCKS = 8
LRU_BLOCK_DIM = LRU_WIDTH // LRU_BLOCKS
CONV_WIDTH = 4
RG_C = 8.0
RET_HEADS = 8
RET_HEAD_DIM = 64
RET_WIDTH = RET_HEADS * RET_HEAD_DIM
CHUNK = 128
ROPE_BASE = 10000.0
D_FF = 4 * D_MODEL
EPS = 1e-6

SUBLANES = 8
LANES = 128
BF16_ROWS = 16
VMEM_LIMIT_BYTES = 56 * 1024 * 1024

HEAD_PAIRS = RET_WIDTH // LANES
HALF = RET_HEAD_DIM // 2

OFF_XA = 0
OFF_GA = OFF_XA + LRU_WIDTH
OFF_Q = OFF_GA + LRU_WIDTH
OFF_K = OFF_Q + RET_WIDTH
OFF_V = OFF_K + RET_WIDTH
OFF_GR = OFF_V + RET_WIDTH
OFF_SA = OFF_GR + RET_WIDTH
OFF_SB = OFF_SA + D_MODEL
D_IN = OFF_SB + D_MODEL

SEQ_TILE = 512
FFN_TILE = 1024
STAGE_BYTES = 2 * 1024 * 1024
IN_STAGE_ROWS = 64
FFN_ROWS = 512
FF_BLOCK = 1024
SCORE_LOOKAHEAD = 1


def _dot(a, b):
    return jnp.dot(a, b, preferred_element_type=jnp.float32)


def _pack_rows(w):
    k, n = w.shape
    w = w.astype(jnp.bfloat16).reshape(k // BF16_ROWS, BF16_ROWS, n // LANES, LANES)
    return jnp.swapaxes(w, 1, 2)


def _unpack_rows(tiles):
    return jnp.concatenate(
        [jnp.concatenate([tiles[i, j] for j in range(tiles.shape[1])], axis=1)
         for i in range(tiles.shape[0])], axis=0)


def _rms(x, g):
    return x * lax.rsqrt(jnp.mean(x * x, axis=-1, keepdims=True) + EPS) * g


NEG_LOG2_E = np.float32(-np.log2(np.e))


def _exp_neg(x):
    return jnp.exp2(x * NEG_LOG2_E)


def _sigmoid(x):
    return 1.0 / (1.0 + _exp_neg(x))


def _gelu_tanh_times(x, y):
    c = np.float32(np.sqrt(2.0 / np.pi))
    return (0.5 * x * (1.0 + jnp.tanh(c * (x + 0.044715 * (x * x * x))))) * y


def _mixer_kernel(x_ref, g1_ref, win_hbm, convw_ref, convb_ref, wg_ref, br_ref, bi_ref,
                  lam_ref, wa_hbm, wb_hbm, wout_hbm, cos_ref, sin_ref, dec_ref, qdec_ref,
                  kdec_ref, cd_ref, smask_ref, o_ref,
                  xa_ext, hcar, state, yb_s, win_ref, wa_ref, wb_ref, wout_ref,
                  stage_in, stage_sq, sem, *, layer):
    ts = x_ref.shape[0]

    @pl.when((pl.program_id(0) == 0) & (pl.program_id(1) == 0))
    def _():
        qk_tiles = tuple(range(OFF_Q // LANES, OFF_V // LANES))
        _ingest_weight(win_hbm.at[layer], win_ref, stage_in, sem, pair_layout_cols=qk_tiles)
        _ingest_weight(wa_hbm.at[layer], wa_ref, stage_sq, sem)
        _ingest_weight(wb_hbm.at[layer], wb_ref, stage_sq, sem)
        _ingest_weight(wout_hbm.at[layer], wout_ref, stage_sq, sem)

    @pl.when(pl.program_id(1) == 0)
    def _():
        xa_ext[0:SUBLANES, :] = jnp.zeros((SUBLANES, LRU_WIDTH), jnp.float32)
        hcar[...] = jnp.zeros_like(hcar)
        state[...] = jnp.zeros_like(state)

    x = x_ref[...]
    hb = _rms(x, g1_ref[...]).astype(jnp.bfloat16)

    def proj(off, n):
        return _dot(hb, _unpack_rows(win_ref[:, off // LANES:(off + n) // LANES]))

    xa = proj(OFF_XA, LRU_WIDTH)
    k_all = proj(OFF_K, RET_WIDTH)

    xa_ext[SUBLANES:SUBLANES + ts, :] = xa
    xc = convb_ref[...] + convw_ref[CONV_WIDTH - 1:CONV_WIDTH, :] * xa
    for j in range(CONV_WIDTH - 1):
        lo = SUBLANES - (CONV_WIDTH - 1) + j
        xc = xc + convw_ref[j:j + 1, :] * xa_ext[lo:lo + ts, :]
    xa_ext[0:SUBLANES, :] = xa[ts - SUBLANES:ts, :]
    xcb = xc.astype(jnp.bfloat16)
    half = LRU_WIDTH // 2
    groups = CHUNK // SUBLANES
    z = -lam_ref[...]
    decay_rate = RG_C * (jnp.maximum(z, 0.0) + jnp.log1p(jnp.exp(-jnp.abs(z))))
    row = lax.broadcasted_iota(jnp.int32, (groups, SUBLANES, LRU_WIDTH), 1)

    def lru_chunk(c, carry):
        rows = slice(c * CHUNK, (c + 1) * CHUNK)
        g0 = _dot(xcb[rows, :half], _unpack_rows(wg_ref[0]))
        g1 = _dot(xcb[rows, half:], _unpack_rows(wg_ref[1]))
        r = _sigmoid(jnp.concatenate([g0[:, :half], g1[:, :half]], axis=1) + br_ref[...])
        gate_i = _sigmoid(jnp.concatenate([g0[:, half:], g1[:, half:]], axis=1) + bi_ref[...])
        neg_log_a = r * decay_rate
        a = _exp_neg(neg_log_a)
        one_minus_a2 = jnp.tanh(neg_log_a) * (a * a + 1.0)
        mult = jnp.where(one_minus_a2 > 0.0, one_minus_a2 * lax.rsqrt(one_minus_a2), 0.0)
        u = mult * (gate_i * xc[rows])

        a3 = a.reshape(groups, SUBLANES, LRU_WIDTH)
        u3 = u.reshape(groups, SUBLANES, LRU_WIDTH)
        shift = 1
        while shift < SUBLANES:
            keep = row >= shift
            a_prev = jnp.where(keep, pltpu.roll(a3, shift, 1), 1.0)
            u_prev = jnp.where(keep, pltpu.roll(u3, shift, 1), 0.0)
            u3 = a3 * u_prev + u3
            a3 = a3 * a_prev
            shift *= 2
        h_groups = []
        for g in range(groups):
            hg = u3[g] + a3[g] * carry
            h_groups.append(hg)
            carry = jnp.broadcast_to(hg[SUBLANES - 1:SUBLANES, :], (SUBLANES, LRU_WIDTH))
        return jnp.concatenate(h_groups, axis=0), carry

    slice_w = 2 * LANES
    n_chunks = ts // CHUNK
    wide_offs = [off + j * slice_w for off in (OFF_Q, OFF_V, OFF_GR, OFF_GA)
                 for j in range(RET_WIDTH // slice_w)]
    assert ts % CHUNK == 0 and len(wide_offs) % n_chunks == 0
    per_chunk = len(wide_offs) // n_chunks
    carry =jnp.broadcast_to(hcar[...], (SUBLANES, LRU_WIDTH))
    h_chunks = []
    wide = []
    for c in range(n_chunks):
        h_c, carry = lru_chunk(c, carry)
        h_chunks.append(h_c)
        wide += [proj(off, slice_w) for off in wide_offs[c * per_chunk:(c + 1) * per_chunk]]
    hcar[...] = carry[0:1, :]
    n_sl = RET_WIDTH // slice_w
    q_all, v_all, gr_all, ga = (jnp.concatenate(wide[j * n_sl:(j + 1) * n_sl], axis=1)
                                for j in range(4))
    ya = _gelu_tanh_times(ga, jnp.concatenate(h_chunks, axis=0)).astype(jnp.bfloat16)

    lane = lax.broadcasted_iota(jnp.int32, (1, LANES), 1)
    qk_head_a = (lane // HALF) % 2 == 0
    v_head_a = lane < RET_HEAD_DIM
    cosv = cos_ref[...]
    sinv = sin_ref[...]
    tasks = [(c, p) for c in range(ts // CHUNK) for p in range(HEAD_PAIRS)]

    def rotate(t_all, p):
        tp = t_all[:, p * LANES:(p + 1) * LANES]
        return tp * cosv + pltpu.roll(tp, LANES // 2, 1) * sinv

    q_rot = [rotate(q_all, p) for p in range(HEAD_PAIRS)]
    k_rot = [rotate(k_all, p) for p in range(HEAD_PAIRS)]
    st = [state[p] for p in range(HEAD_PAIRS)]

    def scores_of(c, p):
        rows = slice(c * CHUNK, (c + 1) * CHUNK)
        kc = k_rot[p][rows]
        k2 = jnp.concatenate([jnp.where(qk_head_a, kc, 0.0), jnp.where(qk_head_a, 0.0, kc)],
                             axis=0).astype(jnp.bfloat16)
        return lax.dot_general(q_rot[p][rows].astype(jnp.bfloat16), k2, (((1,), (1,)), ((), ())),
                               preferred_element_type=jnp.float32) * dec_ref[p]

    n_wide = D_MODEL // slice_w
    fillers = [functools.partial(proj, off + j * slice_w, slice_w)
               for off in (OFF_SA, OFF_SB) for j in range(n_wide)]
    filled = []
    assert len(tasks) % len(fillers) == 0
    stride = len(tasks) // len(fillers)

    pending = [scores_of(*t) for t in tasks[:SCORE_LOOKAHEAD]]
    for i, (c, p) in enumerate(tasks):
        scores = pending.pop(0)
        rows = slice(c * CHUNK, (c + 1) * CHUNK)
        lanes_p = slice(p * LANES, (p + 1) * LANES)
        qc, kc, vc = q_rot[p][rows], k_rot[p][rows], v_all[rows, lanes_p]
        vcb = vc.astype(jnp.bfloat16)
        v2 = jnp.concatenate([jnp.where(v_head_a, vc, 0.0), jnp.where(v_head_a, 0.0, vc)],
                             axis=0).astype(jnp.bfloat16)
        lhs = jnp.concatenate([scores, qc * qdec_ref[p]], axis=1).astype(jnp.bfloat16)
        rhs = jnp.concatenate([v2, st[p].astype(jnp.bfloat16)], axis=0)
        if i + SCORE_LOOKAHEAD < len(tasks):
            pending.append(scores_of(*tasks[i + SCORE_LOOKAHEAD]))
        if i % stride == stride - 1:
            filled.append(fillers.pop(0)())
        ret = jnp.dot(lhs, rhs, preferred_element_type=jnp.float32)
        kv = lax.dot_general((kc * kdec_ref[p]).astype(jnp.bfloat16), vcb,
                             (((0,), (0,)), ((), ())), preferred_element_type=jnp.float32)
        st[p] = cd_ref[p] * st[p] + smask_ref[...] * kv
        sq = ret * ret
        sum_a = jnp.sum(jnp.where(v_head_a, sq, 0.0), axis=-1, keepdims=True)
        sum_b = jnp.sum(jnp.where(v_head_a, 0.0, sq), axis=-1, keepdims=True)
        ms = jnp.where(v_head_a, sum_a, sum_b) * (1.0 / RET_HEAD_DIM)
        gr = gr_all[rows, lanes_p]
        yb_s[rows, lanes_p] = (gr * _sigmoid(gr)) * (ret * lax.rsqrt(ms + EPS))
    assert not fillers
    for p in range(HEAD_PAIRS):
        state[p] = st[p]
    sa, sb = (jnp.concatenate(filled[j * n_wide:(j + 1) * n_wide], axis=1) for j in range(2))

    za = _dot(ya, _unpack_rows(wa_ref[...]))
    zb = _dot(yb_s[...].astype(jnp.bfloat16), _unpack_rows(wb_ref[...]))
    m = _sigmoid(sa) * za + _sigmoid(sb) * zb
    o_ref[...] = x + _dot(m.astype(jnp.bfloat16), _unpack_rows(wout_ref[...]))


def _ingest_weight(w_hbm, w_vmem, stage, sem, pair_layout_cols=()):
    rows = stage.shape[1]
    n_chunks = w_hbm.shape[0] // rows

    def chunk_copy(i):
        return pltpu.make_async_copy(w_hbm.at[pl.ds(i * rows, rows)], stage.at[i % 2], sem.at[i % 2])

    chunk_copy(0).start()
    for i in range(n_chunks):
        if i + 1 < n_chunks:
            chunk_copy(i + 1).start()
        chunk_copy(i).wait()
        if len(w_vmem.shape) == 2:
            w_vmem[i * rows:(i + 1) * rows, :] = stage[i % 2].astype(jnp.bfloat16)
            continue
        lane = lax.broadcasted_iota(jnp.int32, (rows, LANES), 1)
        for nb in range(w_vmem.shape[1]):
            slab = stage[i % 2, :, nb * LANES:(nb + 1) * LANES]
            if nb in pair_layout_cols:
                slab = jnp.where(lane // HALF == 1, pltpu.roll(slab, LANES - HALF, 1),
                                 jnp.where(lane // HALF == 2, pltpu.roll(slab, HALF, 1), slab))
            slab = slab.astype(jnp.bfloat16)
            for rb in range(rows // BF16_ROWS):
                w_vmem[i * rows // BF16_ROWS + rb, nb] = slab[rb * BF16_ROWS:(rb + 1) * BF16_ROWS]


def _ffn_kernel(x_ref, g2_ref, w1_hbm, w2_hbm, gf_ref, o_ref, w1_s, w2_s, stage1, stage2, sem,
                *, layer, final_norm):
    @pl.when(pl.program_id(0) == 0)
    def _():
        _ingest_weight(w1_hbm.at[layer], w1_s, stage1, sem)
        _ingest_weight(w2_hbm.at[layer], w2_s, stage2, sem)

    for i in range(x_ref.shape[0] // FFN_ROWS):
        rows = slice(i * FFN_ROWS, (i + 1) * FFN_ROWS)
        x = x_ref[rows, :]
        hb = _rms(x, g2_ref[...]).astype(jnp.bfloat16)
        acc = x
        for j in range(D_FF // FF_BLOCK):
            cols = slice(j * FF_BLOCK, (j + 1) * FF_BLOCK)
            t = jnp.maximum(_dot(hb, w1_s[:, cols]), 0.0)
            acc = acc + _dot((t * t).astype(jnp.bfloat16), w2_s[cols, :])
        o_ref[rows, :] = _rms(acc, gf_ref[...]) if final_norm else acc


def _retention_tables(seq):
    log_g = np.log1p(-np.exp2(-5.0 - np.arange(RET_HEADS, dtype=np.float64)))
    pos = np.arange(CHUNK, dtype=np.float64)
    rel = pos[:, None] - pos[None, :]
    intra = np.where(rel[None] >= 0, np.exp(log_g[:, None, None] * np.maximum(rel, 0.0)[None]), 0.0)
    q_decay = np.exp(log_g[:, None] * (pos[None, :] + 1.0))
    k_decay = np.exp(log_g[:, None] * (CHUNK - 1.0 - pos[None, :]))
    chunk_decay = np.exp(log_g * CHUNK)

    lane = np.arange(LANES)
    qk_sel = (lane // HALF) % 2
    v_sel = lane // RET_HEAD_DIM
    pairs = np.arange(HEAD_PAIRS)
    k_scale = RET_HEAD_DIM ** -0.5
    dec = k_scale * np.concatenate([intra[0::2], intra[1::2]], axis=2)
    head_of_lane = 2 * pairs[:, None] + qk_sel[None, :]
    qdec = k_scale * np.transpose(q_decay[head_of_lane], (0, 2, 1))
    kdec = np.transpose(k_decay[head_of_lane], (0, 2, 1))
    smask = (qk_sel[:, None] == v_sel[None, :]).astype(np.float64)
    cd = chunk_decay[head_of_lane][:, :, None] * smask[None]

    inv = ROPE_BASE ** (-np.arange(HALF, dtype=np.float64) / HALF)
    ang = np.arange(seq, dtype=np.float64)[:, None] * inv[None, :]
    cos, sin = np.cos(ang), np.sin(ang)
    cos_t = np.tile(cos, (1, LANES // HALF))
    sin_t = np.concatenate([-sin, -sin, sin, sin], axis=1)
    return tuple(jnp.asarray(t, jnp.float32) for t in (cos_t, sin_t, dec, qdec, kdec, cd, smask))


def _gate_weights(w):
    per_half = LRU_BLOCKS // 2
    rows = w.reshape(2, per_half * LRU_BLOCK_DIM, LRU_BLOCK_DIM)
    tiled = jnp.tile(rows, (1, 1, per_half))
    blk = np.arange(per_half * LRU_BLOCK_DIM) // LRU_BLOCK_DIM
    return tiled * jnp.asarray(blk[:, None] == blk[None, :], w.dtype)


def _tile_scratch(k, n):
    return pltpu.VMEM((k // BF16_ROWS, n // LANES, BF16_ROWS, LANES), jnp.bfloat16)


def _tiled(k, n, lead=()):
    return _full(lead + (k // BF16_ROWS, n // LANES, BF16_ROWS, LANES))


def _full(shape):
    return pl.BlockSpec(shape, lambda *_: (0,) * len(shape), pipeline_mode=pl.Buffered(1))


def kernel(x, norm1_g, w_in, conv_w, conv_b, lru_wr, lru_br, lru_wi, lru_bi, lru_lambda,
           w_branch_a, w_branch_b, w_out, norm2_g, w_ff1, w_ff2, norm_f_g):
    bsz, seq, _ = x.shape
    depth = w_in.shape[0]
    bf = jnp.bfloat16
    cos_t, sin_t, dec, qdec, kdec, cd, smask = _retention_tables(seq)
    ts = SEQ_TILE
    half = LRU_WIDTH // 2

    def row_param(p):
        return p.reshape(depth, 1, -1)

    def layer_row(l, width):
        return pl.BlockSpec((None, 1, width), lambda *_: (l, 0, 0))

    for l in range(depth):
        wg = jnp.concatenate([_gate_weights(lru_wr[l]), _gate_weights(lru_wi[l])], axis=2)
        wg = jnp.stack([_pack_rows(wg[0]), _pack_rows(wg[1])])

        x = pl.pallas_call(
            functools.partial(_mixer_kernel, layer=l),
            name="mixer",
            grid=(bsz, seq // ts),
            in_specs=[
                pl.BlockSpec((None, ts, D_MODEL), lambda b, s: (b, s, 0)),
                layer_row(l, D_MODEL),
                pl.BlockSpec(memory_space=pl.ANY),
                pl.BlockSpec((None, CONV_WIDTH, LRU_WIDTH), lambda *_: (l, 0, 0)),
                layer_row(l, LRU_WIDTH),
                _tiled(half, LRU_WIDTH, lead=(2,)),
                layer_row(l, LRU_WIDTH),
                layer_row(l, LRU_WIDTH),
                layer_row(l, LRU_WIDTH),
                pl.BlockSpec(memory_space=pl.ANY),
                pl.BlockSpec(memory_space=pl.ANY),
                pl.BlockSpec(memory_space=pl.ANY),
                pl.BlockSpec((ts, LANES), lambda b, s: (s, 0)),
                pl.BlockSpec((ts, LANES), lambda b, s: (s, 0)),
                _full((HEAD_PAIRS, CHUNK, 2 * CHUNK)),
                _full((HEAD_PAIRS, CHUNK, LANES)),
                _full((HEAD_PAIRS, CHUNK, LANES)),
                _full((HEAD_PAIRS, LANES, LANES)),
                _full((LANES, LANES)),
            ],
            out_specs=pl.BlockSpec((None, ts, D_MODEL), lambda b, s: (b, s, 0)),
            out_shape=jax.ShapeDtypeStruct((bsz, seq, D_MODEL), jnp.float32),
            scratch_shapes=[
                pltpu.VMEM((SUBLANES + ts, LRU_WIDTH), jnp.float32),
                pltpu.VMEM((1, LRU_WIDTH), jnp.float32),
                pltpu.VMEM((HEAD_PAIRS, LANES, LANES), jnp.float32),
                pltpu.VMEM((ts, RET_WIDTH), jnp.float32),
                _tile_scratch(D_MODEL, D_IN),
                _tile_scratch(LRU_WIDTH, D_MODEL),
                _tile_scratch(RET_WIDTH, D_MODEL),
                _tile_scratch(D_MODEL, D_MODEL),
                pltpu.VMEM((2, IN_STAGE_ROWS, D_IN), jnp.float32),
                pltpu.VMEM((2, STAGE_BYTES // (4 * D_MODEL), D_MODEL), jnp.float32),
                pltpu.SemaphoreType.DMA((2,)),
            ],
            compiler_params=pltpu.CompilerParams(
                dimension_semantics=("arbitrary", "arbitrary"),
                vmem_limit_bytes=VMEM_LIMIT_BYTES),
        )(x, row_param(norm1_g), w_in, conv_w, row_param(conv_b), wg,
          row_param(lru_br), row_param(lru_bi), row_param(lru_lambda),
          w_branch_a, w_branch_b, w_out,
          cos_t, sin_t, dec, qdec, kdec, cd, smask)

        tokens = bsz * seq
        xf = pl.pallas_call(
            functools.partial(_ffn_kernel, layer=l, final_norm=(l == depth - 1)),
            name="ffn",
            grid=(tokens // FFN_TILE,),
            in_specs=[
                pl.BlockSpec((FFN_TILE, D_MODEL), lambda i: (i, 0)),
                layer_row(l, D_MODEL),
                pl.BlockSpec(memory_space=pl.ANY),
                pl.BlockSpec(memory_space=pl.ANY),
                _full((1, D_MODEL)),
            ],
            out_specs=pl.BlockSpec((FFN_TILE, D_MODEL), lambda i: (i, 0)),
            out_shape=jax.ShapeDtypeStruct((tokens, D_MODEL), jnp.float32),
            scratch_shapes=[
                pltpu.VMEM((D_MODEL, D_FF), bf),
                pltpu.VMEM((D_FF, D_MODEL), bf),
                pltpu.VMEM((2, STAGE_BYTES // (4 * D_FF), D_FF), jnp.float32),
                pltpu.VMEM((2, STAGE_BYTES // (4 * D_MODEL), D_MODEL), jnp.float32),
                pltpu.SemaphoreType.DMA((2,)),
            ],
            compiler_params=pltpu.CompilerParams(
                dimension_semantics=("arbitrary",),
                vmem_limit_bytes=VMEM_LIMIT_BYTES),
        )(x.reshape(tokens, D_MODEL), row_param(norm2_g), w_ff1, w_ff2, norm_f_g[None])
        x = xf.reshape(bsz, seq, D_MODEL)
    return x
```
